```python
import jax, jax.numpy as jnp
from jax import lax
import numpy as np

D_MODEL = 1024
BATCH = 4
SEQ = 4096
DEPTH = 2
DEC_BATCH = 32
DEC_SEQ = 8
PAST_LEN = 8192
PAGE_SIZE = 128

N_MIXERS = 2
N_CONV_LAYERS = (DEPTH + N_MIXERS - 1) // N_MIXERS
N_SB_LAYERS = DEPTH // N_MIXERS
CONV_WIDTH = 31
CONV_STATE = CONV_WIDTH - 1
N_HEADS = 16
HEAD_DIM = D_MODEL // N_HEADS
Q_BLOCK = 128
D_FF = -(-8 * D_MODEL // (3 * 256)) * 256
RMS_EPS = 1e-6
LN_EPS = 1e-5
SB_BIAS_INIT = -7.0

kernel_name = "conformer_conv_stickbreaking_hybrid_step"


def rmsnorm(x, g):
    x32 = x.astype(jnp.float32)
    y = x32 * lax.rsqrt(jnp.mean(x32 * x32, axis=-1, keepdims=True) + RMS_EPS)
    return (y * g.astype(jnp.float32)).astype(x.dtype)


def layernorm(x, g, b):
    x32 = x.astype(jnp.float32)
    mu = jnp.mean(x32, axis=-1, keepdims=True)
    var = jnp.mean(jnp.square(x32 - mu), axis=-1, keepdims=True)
    y = (x32 - mu) * lax.rsqrt(var + LN_EPS) * g.astype(jnp.float32) + b.astype(jnp.float32)
    return y.astype(x.dtype)


def swiglu_ffn(h, w_gate, w_up, w_down):
    return (jax.nn.silu(h @ w_gate) * (h @ w_up)) @ w_down


def conformer_conv(h, prefix, w_pw1, b_pw1, w_dw, b_dw, ln_g, ln_b, w_pw2, b_pw2):
    u = h @ w_pw1 + b_pw1
    a, gate = jnp.split(u, 2, axis=-1)
    u = a * jax.nn.sigmoid(gate)
    u_ext = jnp.concatenate([prefix.astype(u.dtype), u], axis=1)
    c = lax.conv_general_dilated(
        u_ext, w_dw[:, None, :].astype(u.dtype), window_strides=(1,), padding='VALID',
        dimension_numbers=('NWC', 'WIO', 'NWC'), feature_group_count=D_MODEL) + b_dw
    c = jax.nn.silu(layernorm(c, ln_g, ln_b))
    return c @ w_pw2 + b_pw2, u_ext[:, -CONV_STATE:]


def sb_attend(q, k, v, q_pos, bias):
    tk = k.shape[1]
    z = jnp.einsum('bthd,bshd->bhts', q.astype(jnp.float32), k.astype(jnp.float32)) * (HEAD_DIM ** -0.5)
    z = z + bias.astype(jnp.float32)[None, :, None, None]
    causal = jnp.arange(tk, dtype=jnp.int32)[None, :] < q_pos[:, None]
    log_1m_beta = jnp.where(causal, -jax.nn.softplus(z), 0.0)
    after = lax.cumsum(log_1m_beta, axis=3, reverse=True) - log_1m_beta
    a = jnp.where(causal, jnp.exp(jax.nn.log_sigmoid(z) + after), 0.0)
    o = jnp.einsum('bhts,bshd->bthd', a, v.astype(jnp.float32))
    return o.astype(q.dtype)


def sb_prompt_attention(q, k, v, bias):
    b, t = q.shape[0], q.shape[1]
    nb = t // Q_BLOCK
    q_blocks = q.reshape(b, nb, Q_BLOCK, N_HEADS, HEAD_DIM).swapaxes(0, 1)
    pos = jnp.arange(t, dtype=jnp.int32).reshape(nb, Q_BLOCK)
    o = lax.map(lambda qp: sb_attend(qp[0], k, v, qp[1], bias), (q_blocks, pos))
    return o.swapaxes(0, 1).reshape(b, t, N_HEADS * HEAD_DIM)


def sb_sample_attention(q, k_new, v_new, k_pool, v_pool, page_table, bias):
    db, n_pages = page_table.shape
    past = n_pages * k_pool.shape[1]
    k_past = k_pool[page_table].reshape(db, past, N_HEADS, HEAD_DIM)
    v_past = v_pool[page_table].reshape(db, past, N_HEADS, HEAD_DIM)
    k_all = jnp.concatenate([k_past.astype(k_new.dtype), k_new], axis=1)
    v_all = jnp.concatenate([v_past.astype(v_new.dtype), v_new], axis=1)
    q_pos = past + jnp.arange(q.shape[1], dtype=jnp.int32)
    return sb_attend(q, k_all, v_all, q_pos, bias).reshape(db, q.shape[1], N_HEADS * HEAD_DIM)


def split_qkv(h, w_qkv):
    b, t = h.shape[0], h.shape[1]
    qkv = (h @ w_qkv).reshape(b, t, 3, N_HEADS, HEAD_DIM)
    return qkv[:, :, 0], qkv[:, :, 1], qkv[:, :, 2]


def setup_inputs(seed: int = 0) -> dict:
    key = jax.random.key(seed)
    ks = jax.random.split(key, 24)
    f32 = jnp.float32
    n_pages = PAST_LEN // PAGE_SIZE
    n_pool = (DEC_BATCH * n_pages * 5) // 4
    nrm = lambda k, shape, s: jax.random.normal(k, shape, f32) * s
    page_table = jax.random.permutation(ks[5], n_pool)[:DEC_BATCH * n_pages].reshape(DEC_BATCH, n_pages).astype(jnp.int32)
    return {
        "x_prompt": nrm(ks[0], (BATCH, SEQ, D_MODEL), 1.0),
        "x_sample": nrm(ks[1], (DEC_BATCH, DEC_SEQ, D_MODEL), 1.0),
        "cache_conv": nrm(ks[2], (N_CONV_LAYERS, DEC_BATCH, CONV_STATE, D_MODEL), 0.5),
        "cache_k": nrm(ks[3], (N_SB_LAYERS, n_pool, PAGE_SIZE, N_HEADS, HEAD_DIM), 1.0),
        "cache_v": nrm(ks[4], (N_SB_LAYERS, n_pool, PAGE_SIZE, N_HEADS, HEAD_DIM), 1.0),
        "page_table": page_table,
        "mix_norm_g": 1.0 + nrm(ks[6], (DEPTH, D_MODEL), 0.05),
        "ffn_norm_g": 1.0 + nrm(ks[7], (DEPTH, D_MODEL), 0.05),
        "final_norm_g": 1.0 + nrm(ks[8], (D_MODEL,), 0.05),
        "cv_w_pw1": nrm(ks[9], (N_CONV_LAYERS, D_MODEL, 2 * D_MODEL), D_MODEL ** -0.5),
        "cv_b_pw1": nrm(ks[10], (N_CONV_LAYERS, 2 * D_MODEL), 0.02),
        "cv_w_dw": nrm(ks[11], (N_CONV_LAYERS, CONV_WIDTH, D_MODEL), CONV_WIDTH ** -0.5),
        "cv_b_dw": nrm(ks[12], (N_CONV_LAYERS, D_MODEL), 0.02),
        "cv_ln_g": 1.0 + nrm(ks[13], (N_CONV_LAYERS, D_MODEL), 0.05),
        "cv_ln_b": nrm(ks[14], (N_CONV_LAYERS, D_MODEL), 0.02),
        "cv_w_pw2": nrm(ks[15], (N_CONV_LAYERS, D_MODEL, D_MODEL), D_MODEL ** -0.5),
        "cv_b_pw2": nrm(ks[16], (N_CONV_LAYERS, D_MODEL), 0.02),
        "sb_w_qkv": nrm(ks[17], (N_SB_LAYERS, D_MODEL, 3 * N_HEADS * HEAD_DIM), D_MODEL ** -0.5),
        "sb_w_o": nrm(ks[18], (N_SB_LAYERS, N_HEADS * HEAD_DIM, D_MODEL), (N_HEADS * HEAD_DIM) ** -0.5),
        "sb_logit_bias": SB_BIAS_INIT + nrm(ks[22], (N_SB_LAYERS, N_HEADS), 0.1),
        "ffn_w_gate": nrm(ks[19], (DEPTH, D_MODEL, D_FF), D_MODEL ** -0.5),
        "ffn_w_up": nrm(ks[20], (DEPTH, D_MODEL, D_FF), D_MODEL ** -0.5),
        "ffn_w_down": nrm(ks[21], (DEPTH, D_FF, D_MODEL), D_FF ** -0.5),
    }


def reference(x_prompt, x_sample, cache_conv, cache_k, cache_v, page_table,
              mix_norm_g, ffn_norm_g, final_norm_g,
              cv_w_pw1, cv_b_pw1, cv_w_dw, cv_b_dw, cv_ln_g, cv_ln_b, cv_w_pw2, cv_b_pw2,
              sb_w_qkv, sb_w_o, sb_logit_bias, ffn_w_gate, ffn_w_up, ffn_w_down):
    xp, xs = x_prompt, x_sample
    bp, tp = xp.shape[0], xp.shape[1]
    conv_p, conv_s, kp_list, vp_list, ks_list, vs_list = [], [], [], [], [], []
    for i in range(DEPTH):
        j = i // N_MIXERS
        hp = rmsnorm(xp, mix_norm_g[i])
        hs = rmsnorm(xs, mix_norm_g[i])
        if i % N_MIXERS == 0:
            cv = (cv_w_pw1[j], cv_b_pw1[j], cv_w_dw[j], cv_b_dw[j], cv_ln_g[j], cv_ln_b[j], cv_w_pw2[j], cv_b_pw2[j])
            fresh = jnp.zeros((bp, CONV_STATE, D_MODEL), xp.dtype)
            op, st_p = conformer_conv(hp, fresh, *cv)
            os_, st_s = conformer_conv(hs, cache_conv[j], *cv)
            conv_p.append(st_p)
            conv_s.append(st_s)
        else:
            q_p, k_p, v_p = split_qkv(hp, sb_w_qkv[j])
            op = sb_prompt_attention(q_p, k_p, v_p, sb_logit_bias[j]) @ sb_w_o[j]
            kp_list.append(k_p.reshape(bp, tp // PAGE_SIZE, PAGE_SIZE, N_HEADS, HEAD_DIM))
            vp_list.append(v_p.reshape(bp, tp // PAGE_SIZE, PAGE_SIZE, N_HEADS, HEAD_DIM))
            q_s, k_s, v_s = split_qkv(hs, sb_w_qkv[j])
            os_ = sb_sample_attention(q_s, k_s, v_s, cache_k[j], cache_v[j], page_table, sb_logit_bias[j]) @ sb_w_o[j]
            ks_list.append(k_s)
            vs_list.append(v_s)
        xp = xp + op
        xs = xs + os_
        xp = xp + swiglu_ffn(rmsnorm(xp, ffn_norm_g[i]), ffn_w_gate[i], ffn_w_up[i], ffn_w_down[i])
        xs = xs + swiglu_ffn(rmsnorm(xs, ffn_norm_g[i]), ffn_w_gate[i], ffn_w_up[i], ffn_w_down[i])
    y_prompt = rmsnorm(xp, final_norm_g)
    y_sample = rmsnorm(xs, final_norm_g)
    return (y_prompt, y_sample, jnp.stack(conv_p), jnp.stack(conv_s),
            jnp.stack(kp_list), jnp.stack(vp_list), jnp.stack(ks_list), jnp.stack(vs_list))
```

```python
import functools

import jax
import jax.numpy as jnp
from jax import lax
from jax.experimental import pallas as pl
from jax.experimental.pallas import tpu as pltpu

N_HEADS = 16
HEAD_DIM = 64
CONV_WIDTH = 31
CONV_STATE = CONV_WIDTH - 1
PAGE_SIZE = 128
RMS_EPS = 1e-6
LN_EPS = 1e-5

F32 = jnp.float32
BF16 = jnp.bfloat16

LANES = 128
SUBLANES = 8
MXU_DIM = 256
HALO = 32
VMEM_LIMIT = 56 * 1024 * 1024


def _resident(shape):
    nd = len(shape)
    return pl.BlockSpec(shape, lambda *_: (0,) * nd, pipeline_mode=pl.Buffered(1))


def _rmsnorm(x, g):
    return x * lax.rsqrt(jnp.mean(x * x, axis=-1, keepdims=True) + RMS_EPS) * g


def _sigmoid(x):
    return 1.0 / (1.0 + jnp.exp(-x))


def _dot(a, b):
    return jnp.dot(a, b, preferred_element_type=F32)


def _dot_nt(a, b):
    return lax.dot_general(a, b, (((1,), (1,)), ((), ())), preferred_element_type=F32)


def _dot_tn(a, b):
    return lax.dot_general(a, b, (((0,), (0,)), ((), ())), preferred_element_type=F32)


def _neg_softplus(z):
    return -(jnp.maximum(z, 0.0) + jnp.log(1.0 + jnp.exp(-jnp.abs(z))))


def _split_bf16(x):
    hi = x.astype(BF16)
    lo = (x - hi.astype(F32)).astype(BF16)
    return hi, lo


def _glu(h, w1_ref, b1_ref, d):
    a = _dot(h, w1_ref[:, :d]) + b1_ref[:, :d]
    gate = _dot(h, w1_ref[:, d:]) + b1_ref[:, d:]
    return a * _sigmoid(gate)


def _ln_silu_pw2(c, lng_ref, lnb_ref, w2_ref, b2_ref):
    mu = jnp.mean(c, axis=-1, keepdims=True)
    cc = c - mu
    var = jnp.mean(cc * cc, axis=-1, keepdims=True)
    y = cc * lax.rsqrt(var + LN_EPS) * lng_ref[...] + lnb_ref[...]
    y = y * _sigmoid(y)
    return _dot(y.astype(BF16), w2_ref[...]) + b2_ref[...]


def _conv_prompt_kernel(x_ref, g_ref, w1_ref, b1_ref, wdw_ref, bdw_ref, lng_ref, lnb_ref, w2_ref, b2_ref,
                        y_ref, st_ref, ubuf_ref, cbuf_ref, *, rows, cols):
    tt, d = x_ref.shape

    @pl.when(pl.program_id(1) == 0)
    def _():
        ubuf_ref[0:HALO, :] = jnp.zeros((HALO, d), F32)

    x = x_ref[...]
    h = _rmsnorm(x, g_ref[...]).astype(BF16)
    ubuf_ref[HALO:HALO + tt, :] = _glu(h, w1_ref, b1_ref, d)

    off = HALO - CONV_STATE
    for c0 in range(0, d, cols):
        for r0 in range(0, tt, rows):
            acc = jnp.broadcast_to(bdw_ref[:, c0:c0 + cols], (rows, cols))
            for phase in range(SUBLANES):
                taps = [m for m in range(phase, HALO + 1, SUBLANES) if 0 <= m - off < CONV_WIDTH]
                if not taps:
                    continue
                slab = ubuf_ref[r0 + phase:r0 + phase + rows + taps[-1] - phase, c0:c0 + cols]
                for m in taps:
                    w = wdw_ref[m - off, :, c0:c0 + cols]
                    seg = slab[m - phase:m - phase + rows].reshape(rows // SUBLANES, SUBLANES, cols)
                    acc = acc + (seg * w[None]).reshape(rows, cols)
            cbuf_ref[r0:r0 + rows, c0:c0 + cols] = acc

    y_ref[...] = x + _ln_silu_pw2(cbuf_ref[...], lng_ref, lnb_ref, w2_ref, b2_ref)
    st_ref[...] = ubuf_ref[tt + off:tt + HALO, :]
    ubuf_ref[0:HALO, :] = ubuf_ref[tt:tt + HALO, :]


def _conv_sample_kernel(x_ref, cache_ref, g_ref, w1_ref, b1_ref, wdw_ref, bdw_ref, lng_ref, lnb_ref, w2_ref, b2_ref,
                        y_ref, st_ref, ext_ref, cbuf_ref, *, cols):
    n, d = x_ref.shape
    nb, ts = ext_ref.shape[0], ext_ref.shape[1] - HALO
    off = HALO - CONV_STATE
    x = x_ref[...]
    h = _rmsnorm(x, g_ref[...]).astype(BF16)
    ext_ref[:, off:HALO, :] = cache_ref[...]
    ext_ref[:, HALO:, :] = _glu(h, w1_ref, b1_ref, d).reshape(nb, ts, d)
    for c0 in range(0, d, cols):
        acc = jnp.broadcast_to(bdw_ref[:, c0:c0 + cols][None], (nb, ts, cols))
        for k in range(CONV_WIDTH):
            acc = acc + ext_ref[:, off + k:off + k + ts, c0:c0 + cols] * wdw_ref[k, :, c0:c0 + cols][None]
        cbuf_ref[:, c0:c0 + cols] = acc.reshape(n, cols)
    y_ref[...] = x + _ln_silu_pw2(cbuf_ref[...], lng_ref, lnb_ref, w2_ref, b2_ref)
    st_ref[...] = ext_ref[:, ts + off:, :]


def _conv_weights(g, w1, b1, wdw, bdw, lng, lnb, w2, b2):
    d = w2.shape[0]
    row = lambda v: v.reshape(1, -1).astype(F32)
    wdw8 = jnp.broadcast_to(wdw[:, None, :], (CONV_WIDTH, SUBLANES, d))
    return (row(g), w1.astype(BF16), row(b1), wdw8, row(bdw), row(lng), row(lnb), w2.astype(BF16), row(b2))


def _conv_weight_specs(d):
    return [_resident((1, d)), _resident((d, 2 * d)), _resident((1, 2 * d)), _resident((CONV_WIDTH, SUBLANES, d)),
            _resident((1, d)), _resident((1, d)), _resident((1, d)), _resident((d, d)), _resident((1, d))]


def _conv_prompt(x, weights, *, tile_t):
    b, t, d = x.shape
    tt = min(tile_t, t)
    assert t % tt == 0 and tt % 64 == 0 and d % 256 == 0
    kern = functools.partial(_conv_prompt_kernel, rows=64, cols=256)
    return pl.pallas_call(
        kern,
        grid=(b, t // tt),
        in_specs=[pl.BlockSpec((None, tt, d), lambda i, j: (i, j, 0))] + _conv_weight_specs(d),
        out_specs=[pl.BlockSpec((None, tt, d), lambda i, j: (i, j, 0)),
                   pl.BlockSpec((None, CONV_STATE, d), lambda i, j: (i, 0, 0))],
        out_shape=[jax.ShapeDtypeStruct((b, t, d), F32), jax.ShapeDtypeStruct((b, CONV_STATE, d), F32)],
        scratch_shapes=[pltpu.VMEM((HALO + tt, d), F32), pltpu.VMEM((tt, d), F32)],
        compiler_params=pltpu.CompilerParams(dimension_semantics=("arbitrary", "arbitrary"),
                                             vmem_limit_bytes=VMEM_LIMIT),
        name="conv_prompt",
    )(x, *weights)


def _conv_sample(x, cache, weights):
    nb, ts, d = x.shape
    assert ts == SUBLANES and cache.shape == (nb, CONV_STATE, d)
    n = nb * ts
    kern = functools.partial(_conv_sample_kernel, cols=128)
    y, st = pl.pallas_call(
        kern,
        grid=(1,),
        in_specs=[_resident((n, d)), _resident((nb, CONV_STATE, d))] + _conv_weight_specs(d),
        out_specs=[pl.BlockSpec((n, d), lambda i: (0, 0)), pl.BlockSpec((nb, CONV_STATE, d), lambda i: (0, 0, 0))],
        out_shape=[jax.ShapeDtypeStruct((n, d), F32), jax.ShapeDtypeStruct((nb, CONV_STATE, d), F32)],
        scratch_shapes=[pltpu.VMEM((nb, HALO + ts, d), F32), pltpu.VMEM((n, d), F32)],
        compiler_params=pltpu.CompilerParams(dimension_semantics=("arbitrary",), vmem_limit_bytes=VMEM_LIMIT),
        name="conv_sample",
    )(x.reshape(n, d), cache, *weights)
    return y.reshape(nb, ts, d), st


def _ffn_kernel(*refs, has_proj, has_final, ff_chunk):
    refs = list(refs)
    x_ref = refs.pop(0)
    if has_proj:
        o_ref, wo_ref = refs.pop(0), refs.pop(0)
    g_ref, wg_ref, wu_ref, wd_ref = refs.pop(0), refs.pop(0), refs.pop(0), refs.pop(0)
    if has_final:
        gf_ref = refs.pop(0)
    y_ref, h_ref = refs
    x = x_ref[...]
    if has_proj:
        x = x + _dot(o_ref[...].astype(BF16), wo_ref[...])
    h_ref[...] = _rmsnorm(x, g_ref[...]).astype(BF16)
    acc = x
    d_ff = wg_ref.shape[1]
    for c0 in range(0, d_ff, ff_chunk):
        h = h_ref[...]
        gate = _dot(h, wg_ref[:, c0:c0 + ff_chunk])
        up = _dot(h, wu_ref[:, c0:c0 + ff_chunk])
        act = (gate * _sigmoid(gate) * up).astype(BF16)
        acc = acc + _dot(act, wd_ref[c0:c0 + ff_chunk, :])
    if has_final:
        acc = _rmsnorm(acc, gf_ref[...])
    y_ref[...] = acc


def _ffn(x, g, wg, wu, wd, *, tile_m, proj=None, final_g=None):
    n, d = x.shape
    d_ff = wg.shape[1]
    tm = min(tile_m, n)
    assert n % tm == 0 and d_ff % MXU_DIM == 0
    tok = lambda last, dt=None: pl.BlockSpec((tm, last), lambda i: (i, 0))
    args, specs = [x], [tok(d)]
    if proj is not None:
        o, wo = proj
        args += [o, wo.astype(BF16)]
        specs += [tok(o.shape[1]), _resident(wo.shape)]
    args += [g.reshape(1, d), wg.astype(BF16), wu.astype(BF16), wd.astype(BF16)]
    specs += [_resident((1, d)), _resident((d, d_ff)), _resident((d, d_ff)), _resident((d_ff, d))]
    if final_g is not None:
        args.append(final_g.reshape(1, d))
        specs.append(_resident((1, d)))
    kern = functools.partial(_ffn_kernel, has_proj=proj is not None, has_final=final_g is not None, ff_chunk=MXU_DIM)
    return pl.pallas_call(
        kern,
        grid=(n // tm,),
        in_specs=specs,
        out_specs=tok(d),
        out_shape=jax.ShapeDtypeStruct((n, d), F32),
        scratch_shapes=[pltpu.VMEM((tm, d), BF16)],
        compiler_params=pltpu.CompilerParams(dimension_semantics=("parallel",), vmem_limit_bytes=VMEM_LIMIT),
        name="ffn",
    )(*args)


def _qkv_kernel(x_ref, g_ref, w_ref, *out_refs, with_bf16):
    d = x_ref.shape[1]
    h = _rmsnorm(x_ref[...], g_ref[...]).astype(BF16)
    q = _dot(h, w_ref[:, :d]) * (HEAD_DIM ** -0.5)
    k = _dot(h, w_ref[:, d:2 * d])
    v = _dot(h, w_ref[:, 2 * d:])
    if with_bf16:
        q_ref, k_ref, v_ref, kb_ref, vb_ref = out_refs
        kb_ref[...] = k.astype(BF16)
        vb_ref[...] = v.astype(BF16)
    else:
        q_ref, k_ref, v_ref = out_refs
    q_ref[...] = q.astype(q_ref.dtype)
    k_ref[...] = k
    v_ref[...] = v


def _qkv(x, g, w_qkv, *, tile_m, with_bf16):
    n, d = x.shape
    assert w_qkv.shape == (d, 3 * d) and d == N_HEADS * HEAD_DIM
    tm = min(tile_m, n)
    assert n % tm == 0
    tok = pl.BlockSpec((tm, d), lambda i: (i, 0))
    f32o, bf16o = jax.ShapeDtypeStruct((n, d), F32), jax.ShapeDtypeStruct((n, d), BF16)
    out_shape = [bf16o, f32o, f32o, bf16o, bf16o] if with_bf16 else [f32o, f32o, f32o]
    return pl.pallas_call(
        functools.partial(_qkv_kernel, with_bf16=with_bf16),
        grid=(n // tm,),
        in_specs=[tok, _resident((1, d)), _resident((d, 3 * d))],
        out_specs=[tok] * len(out_shape),
        out_shape=out_shape,
        compiler_params=pltpu.CompilerParams(dimension_semantics=("parallel",), vmem_limit_bytes=VMEM_LIMIT),
        name="qkv",
    )(x, g.reshape(1, d), w_qkv.astype(BF16))


def _sb_prompt_kernel(bias_ref, q_ref, k_ref, v_ref, o_ref, *, blk):
    hp, qi = pl.program_id(1), pl.program_id(2)
    q = q_ref[...]
    lane = lax.broadcasted_iota(jnp.int32, (1, LANES), 1)
    row = lax.broadcasted_iota(jnp.int32, (blk, blk), 0)
    col = lax.broadcasted_iota(jnp.int32, (blk, blk), 1)
    tri = (row > col).astype(BF16)
    causal = col < row
    outs = []
    for hh in range(2):
        in_head = (lane >= HEAD_DIM) if hh else (lane < HEAD_DIM)
        qh = jnp.where(in_head, q, jnp.zeros_like(q))
        bias = bias_ref[2 * hp + hh]

        def block(kb, carry, masked):
            r, acc = carry
            start = pl.multiple_of(kb * blk, blk)
            z = _dot_nt(qh, k_ref[pl.ds(start, blk), :]) + bias
            lg = _neg_softplus(z)
            if masked:
                lg = jnp.where(causal, lg, 0.0)
            hi, lo = _split_bf16(lg)
            after = _dot(hi, tri) + _dot(lo, tri)
            a = jnp.exp(z + lg + after + r)
            if masked:
                a = jnp.where(causal, a, 0.0)
            acc = acc + _dot(a.astype(BF16), v_ref[pl.ds(start, blk), :])
            r = r + jnp.sum(lg, axis=-1, keepdims=True)
            return r, acc

        carry = (jnp.zeros((blk, 1), F32), jnp.zeros((blk, LANES), F32))
        carry = block(qi, carry, True)
        carry = lax.fori_loop(0, qi, lambda j, c: block(qi - 1 - j, c, False), carry)
        outs.append(carry[1])
    o_ref[...] = jnp.where(lane < HEAD_DIM, outs[0], outs[1]).astype(o_ref.dtype)


def _sb_prompt(q, k, v, bias, *, blk):
    b, t, d = q.shape
    blk = min(blk, t)
    assert t % blk == 0 and d == N_HEADS * HEAD_DIM and 2 * HEAD_DIM == LANES
    kv_spec = pl.BlockSpec((None, t, LANES), lambda i, h, j: (i, 0, h))
    q_spec = pl.BlockSpec((None, blk, LANES), lambda i, h, j: (i, j, h))
    return pl.pallas_call(
        functools.partial(_sb_prompt_kernel, blk=blk),
        grid=(b, N_HEADS // 2, t // blk),
        in_specs=[pl.BlockSpec(memory_space=pltpu.SMEM), q_spec, kv_spec, kv_spec],
        out_specs=q_spec,
        out_shape=jax.ShapeDtypeStruct((b, t, d), BF16),
        compiler_params=pltpu.CompilerParams(dimension_semantics=("parallel", "parallel", "arbitrary"),
                                             vmem_limit_bytes=VMEM_LIMIT),
        name="sb_prompt",
    )(bias.astype(F32), q, k, v)


def _sb_sample_kernel(pt_ref, bias_ref, q_ref, kn_ref, vn_ref, kpool_ref, vpool_ref, o_ref,
                      kbuf_ref, vbuf_ref, pad_ref, qbd_ref, r_ref, acc_ref, sem_ref, *, n_chunks, pages):
    n = pl.program_id(0)
    n_steps = pl.num_programs(0)
    ts, d = q_ref.shape
    hq = N_HEADS * ts
    n_pages = n_chunks * pages
    group = MXU_DIM // PAGE_SIZE
    blk = group * PAGE_SIZE

    def copies(step, slot):
        seq = step // n_chunks
        first = (n_chunks - 1 - step % n_chunks) * pages
        out = []
        for p in range(pages):
            page = pt_ref[seq * n_pages + first + p]
            out.append(pltpu.make_async_copy(kpool_ref.at[page], kbuf_ref.at[slot, p], sem_ref.at[0, slot]))
            out.append(pltpu.make_async_copy(vpool_ref.at[page], vbuf_ref.at[slot, p], sem_ref.at[1, slot]))
        return out

    slot = n % 2

    @pl.when(n == 0)
    def _():
        for cp in copies(n, slot):
            cp.start()

    @pl.when(n + 1 < n_steps)
    def _():
        for cp in copies(n + 1, 1 - slot):
            cp.start()

    row = lax.broadcasted_iota(jnp.int32, (blk, blk), 0)
    col = lax.broadcasted_iota(jnp.int32, (blk, blk), 1)
    tri = (row > col).astype(BF16)

    def fold(z, valid, times_v):
        m = z.shape[1]
        lg = _neg_softplus(z)
        if valid is not None:
            lg = jnp.where(valid, lg, 0.0)
        hi, lo = _split_bf16(lg)
        t = tri[:m, :m]
        a = jnp.exp(z + lg + _dot(hi, t) + _dot(lo, t) + r_ref[...])
        if valid is not None:
            a = jnp.where(valid, a, 0.0)
        acc_ref[...] += times_v(a.astype(BF16))
        r_ref[...] += jnp.sum(lg, axis=-1, keepdims=True)

    @pl.when(n % n_chunks == 0)
    def _():
        q = jnp.concatenate([q_ref[...]] * N_HEADS, axis=0)
        qr = lax.broadcasted_iota(jnp.int32, (hq, d), 0)
        qc = lax.broadcasted_iota(jnp.int32, (hq, d), 1)
        qbd_ref[...] = jnp.where(qr // ts == qc // HEAD_DIM, q, 0.0).astype(BF16)
        r_ref[...] = jnp.zeros_like(r_ref)
        acc_ref[...] = jnp.zeros_like(acc_ref)
        pad_ref[...] = jnp.zeros_like(pad_ref)
        pad_ref[0, 0:ts, :] = kn_ref[...]
        pad_ref[1, 0:ts, :] = vn_ref[...]
        qry = lax.broadcasted_iota(jnp.int32, (hq, PAGE_SIZE), 0) % ts
        key = lax.broadcasted_iota(jnp.int32, (hq, PAGE_SIZE), 1)
        z = _dot_nt(qbd_ref[...], pad_ref[0].astype(BF16)) + bias_ref[...]
        fold(z, key < qry, lambda a: _dot(a, pad_ref[1].astype(BF16)))

    for cp in copies(n, slot):
        cp.wait()
    for g in reversed(range(pages // group)):
        kt = jnp.concatenate([kbuf_ref[slot, g * group + i] for i in range(group)], axis=1).astype(BF16)
        z = _dot(qbd_ref[...], kt) + bias_ref[...]

        def times_v(a, g=g):
            vt = jnp.concatenate([vbuf_ref[slot, g * group + i] for i in range(group)], axis=1).astype(BF16)
            return _dot_nt(a, vt)

        fold(z, None, times_v)

    @pl.when(n % n_chunks == n_chunks - 1)
    def _():
        orow = lax.broadcasted_iota(jnp.int32, (hq, d), 0)
        ocol = lax.broadcasted_iota(jnp.int32, (hq, d), 1)
        o = jnp.where(orow // ts == ocol // HEAD_DIM, acc_ref[...], 0.0)
        o_ref[...] = jnp.sum(o.reshape(N_HEADS, ts, d), axis=0)


def _sb_sample(q, k_new, v_new, k_pool, v_pool, page_table, bias, *, pages):
    nb, ts, d = q.shape
    n_pages = page_table.shape[1]
    pages = min(pages, n_pages)
    group = MXU_DIM // PAGE_SIZE
    assert ts == SUBLANES and n_pages % pages == 0 and pages % group == 0 and N_HEADS * ts == LANES
    assert k_pool.shape[1:] == (PAGE_SIZE, N_HEADS, HEAD_DIM)
    n_chunks = n_pages // pages
    n_pool = k_pool.shape[0]
    hq = N_HEADS * ts
    pool_t = lambda a: jnp.transpose(a, (0, 2, 3, 1)).reshape(n_pool, d, PAGE_SIZE)
    tok = pl.BlockSpec((None, ts, d), lambda i, pt: (i // n_chunks, 0, 0))
    grid_spec = pltpu.PrefetchScalarGridSpec(
        num_scalar_prefetch=1,
        grid=(nb * n_chunks,),
        in_specs=[pl.BlockSpec((hq, 1), lambda i, pt: (0, 0)), tok, tok, tok,
                  pl.BlockSpec(memory_space=pl.ANY), pl.BlockSpec(memory_space=pl.ANY)],
        out_specs=tok,
        scratch_shapes=[pltpu.VMEM((2, pages, d, PAGE_SIZE), F32), pltpu.VMEM((2, pages, d, PAGE_SIZE), F32),
                        pltpu.VMEM((2, PAGE_SIZE, d), F32), pltpu.VMEM((hq, d), BF16),
                        pltpu.VMEM((hq, 1), F32), pltpu.VMEM((hq, d), F32),
                        pltpu.SemaphoreType.DMA((2, 2))],
    )
    return pl.pallas_call(
        functools.partial(_sb_sample_kernel, n_chunks=n_chunks, pages=pages),
        grid_spec=grid_spec,
        out_shape=jax.ShapeDtypeStruct((nb, ts, d), F32),
        compiler_params=pltpu.CompilerParams(dimension_semantics=("arbitrary",), vmem_limit_bytes=VMEM_LIMIT),
        name="sb_sample",
    )(page_table.reshape(-1), jnp.repeat(bias.astype(F32), ts).reshape(hq, 1), q, k_new, v_new,
      pool_t(k_pool), pool_t(v_pool))


def kernel(x_prompt, x_sample, cache_conv, cache_k, cache_v, page_table, mix_norm_g, ffn_norm_g, final_norm_g,
           cv_w_pw1, cv_b_pw1, cv_w_dw, cv_b_dw, cv_ln_g, cv_ln_b, cv_w_pw2, cv_b_pw2,
           sb_w_qkv, sb_w_o, sb_logit_bias, ffn_w_gate, ffn_w_up, ffn_w_down):
    bp, tp, d = x_prompt.shape
    bs, ts, _ = x_sample.shape
    assert mix_norm_g.shape[0] == 2 and tp % PAGE_SIZE == 0

    cw = _conv_weights(mix_norm_g[0], cv_w_pw1[0], cv_b_pw1[0], cv_w_dw[0], cv_b_dw[0], cv_ln_g[0], cv_ln_b[0],
                       cv_w_pw2[0], cv_b_pw2[0])
    xp, st_p = _conv_prompt(x_prompt, cw, tile_t=256)
    xs, st_s = _conv_sample(x_sample, cache_conv[0], cw)
    ffn0 = (ffn_norm_g[0], ffn_w_gate[0], ffn_w_up[0], ffn_w_down[0])
    xp = _ffn(xp.reshape(bp * tp, d), *ffn0, tile_m=512)
    xs = _ffn(xs.reshape(bs * ts, d), *ffn0, tile_m=512)

    qp, kp, vp, kpb, vpb = _qkv(xp, mix_norm_g[1], sb_w_qkv[0], tile_m=512, with_bf16=True)
    qs, ks, vs = _qkv(xs, mix_norm_g[1], sb_w_qkv[0], tile_m=512, with_bf16=False)
    seq = lambda a: a.reshape(bp, tp, d)
    op = _sb_prompt(seq(qp), seq(kpb), seq(vpb), sb_logit_bias[0], blk=256)
    dec = lambda a: a.reshape(bs, ts, d)
    os_ = _sb_sample(dec(qs), dec(ks), dec(vs), cache_k[0], cache_v[0], page_table, sb_logit_bias[0], pages=8)
    ffn1 = (ffn_norm_g[1], ffn_w_gate[1], ffn_w_up[1], ffn_w_down[1])
    yp = _ffn(xp, *ffn1, tile_m=512, proj=(op.reshape(bp * tp, d), sb_w_o[0]), final_g=final_norm_g)
    ys = _ffn(xs, *ffn1, tile_m=512, proj=(os_.reshape(bs * ts, d), sb_w_o[0]), final_g=final_norm_g)

    pages_p = (1, bp, tp // PAGE_SIZE, PAGE_SIZE, N_HEADS, HEAD_DIM)
    new_s = (1, bs, ts, N_HEADS, HEAD_DIM)
    return (yp.reshape(bp, tp, d), ys.reshape(bs, ts, d), st_p[None], st_s[None],
            kp.reshape(pages_p), vp.reshape(pages_p), ks.reshape(new_s), vs.reshape(new_s))
```

```python
import functools

import jax
import jax.numpy as jnp
from jax import lax
from jax.experimental import pallas as pl
from jax.experimental.pallas import tpu as pltpu

N_HEADS = 16
HEAD_DIM = 64
CONV_WIDTH = 31
CONV_STATE = CONV_WIDTH - 1
PAGE_SIZE = 128
RMS_EPS = 1e-6
LN_EPS = 1e-5
LOG2_E = 1.4426950408889634

F32 = jnp.float32
BF16 = jnp.bfloat16

LANES = 128
SUBLANES = 8
MXU_DIM = 256
HALO = 32
VMEM_LIMIT = 56 * 1024 * 1024


def _resident(shape):
    nd = len(shape)
    return pl.BlockSpec(shape, lambda *_: (0,) * nd, pipeline_mode=pl.Buffered(1))


def _rmsnorm(x, g):
    return x * lax.rsqrt(jnp.mean(x * x, axis=-1, keepdims=True) + RMS_EPS) * g


def _sigmoid(x):
    return 1.0 / (1.0 + jnp.exp(-x))


def _dot(a, b):
    return jnp.dot(a, b, preferred_element_type=F32)


def _dot_nt(a, b):
    return lax.dot_general(a, b, (((1,), (1,)), ((), ())), preferred_element_type=F32)


def _dot_tn(a, b):
    return lax.dot_general(a, b, (((0,), (0,)), ((), ())), preferred_element_type=F32)


def _neg_softplus(z):
    return -(jnp.maximum(z, 0.0) + jnp.log(1.0 + jnp.exp(-jnp.abs(z))))


def _split_bf16(x):
    hi = x.astype(BF16)
    lo = (x - hi.astype(F32)).astype(BF16)
    return hi, lo


def _glu(h, w1_ref, b1_ref, d):
    a = _dot(h, w1_ref[:, :d]) + b1_ref[:, :d]
    gate = _dot(h, w1_ref[:, d:]) + b1_ref[:, d:]
    return a * _sigmoid(gate)


def _ln_silu_pw2(c, lng_ref, lnb_ref, w2_ref, b2_ref):
    mu = jnp.mean(c, axis=-1, keepdims=True)
    cc = c - mu
    var = jnp.mean(cc * cc, axis=-1, keepdims=True)
    y = cc * lax.rsqrt(var + LN_EPS) * lng_ref[...] + lnb_ref[...]
    y = y * _sigmoid(y)
    return _dot(y.astype(BF16), w2_ref[...]) + b2_ref[...]


def _conv_prompt_kernel(x_ref, g_ref, w1_ref, b1_ref, wdw_ref, bdw_ref, lng_ref, lnb_ref, w2_ref, b2_ref,
                        y_ref, st_ref, ubuf_ref, cbuf_ref, *, rows, cols):
    tt, d = x_ref.shape

    @pl.when(pl.program_id(1) == 0)
    def _():
        ubuf_ref[0:HALO, :] = jnp.zeros((HALO, d), F32)

    x = x_ref[...]
    h = _rmsnorm(x, g_ref[...]).astype(BF16)
    ubuf_ref[HALO:HALO + tt, :] = _glu(h, w1_ref, b1_ref, d)

    off = HALO - CONV_STATE
    for c0 in range(0, d, cols):
        for r0 in range(0, tt, rows):
            acc = jnp.broadcast_to(bdw_ref[:, c0:c0 + cols], (rows, cols))
            for phase in range(SUBLANES):
                taps = [m for m in range(phase, HALO + 1, SUBLANES) if 0 <= m - off < CONV_WIDTH]
                if not taps:
                    continue
                slab = ubuf_ref[r0 + phase:r0 + phase + rows + taps[-1] - phase, c0:c0 + cols]
                for m in taps:
                    w = wdw_ref[m - off, :, c0:c0 + cols]
                    seg = slab[m - phase:m - phase + rows].reshape(rows // SUBLANES, SUBLANES, cols)
                    acc = acc + (seg * w[None]).reshape(rows, cols)
            cbuf_ref[r0:r0 + rows, c0:c0 + cols] = acc

    y_ref[...] = x + _ln_silu_pw2(cbuf_ref[...], lng_ref, lnb_ref, w2_ref, b2_ref)
    st_ref[...] = ubuf_ref[tt + off:tt + HALO, :]
    ubuf_ref[0:HALO, :] = ubuf_ref[tt:tt + HALO, :]


def _conv_sample_kernel(x_ref, cache_ref, g_ref, w1_ref, b1_ref, wdw_ref, bdw_ref, lng_ref, lnb_ref, w2_ref, b2_ref,
                        y_ref, st_ref, ext_ref, cbuf_ref, *, cols):
    n, d = x_ref.shape
    nb, ts = ext_ref.shape[0], ext_ref.shape[1] - HALO
    off = HALO - CONV_STATE
    x = x_ref[...]
    h = _rmsnorm(x, g_ref[...]).astype(BF16)
    ext_ref[:, off:HALO, :] = cache_ref[...]
    ext_ref[:, HALO:, :] = _glu(h, w1_ref, b1_ref, d).reshape(nb, ts, d)
    for c0 in range(0, d, cols):
        acc = jnp.broadcast_to(bdw_ref[:, c0:c0 + cols][None], (nb, ts, cols))
        for k in range(CONV_WIDTH):
            acc = acc + ext_ref[:, off + k:off + k + ts, c0:c0 + cols] * wdw_ref[k, :, c0:c0 + cols][None]
        cbuf_ref[:, c0:c0 + cols] = acc.reshape(n, cols)
    y_ref[...] = x + _ln_silu_pw2(cbuf_ref[...], lng_ref, lnb_ref, w2_ref, b2_ref)
    st_ref[...] = ext_ref[:, ts + off:, :]


def _conv_weights(g, w1, b1, wdw, bdw, lng, lnb, w2, b2):
    d = w2.shape[0]
    row = lambda v: v.reshape(1, -1).astype(F32)
    wdw8 = jnp.broadcast_to(wdw[:, None, :], (CONV_WIDTH, SUBLANES, d))
    return (row(g), w1.astype(BF16), row(b1), wdw8, row(bdw), row(lng), row(lnb), w2.astype(BF16), row(b2))


def _conv_weight_specs(d):
    return [_resident((1, d)), _resident((d, 2 * d)), _resident((1, 2 * d)), _resident((CONV_WIDTH, SUBLANES, d)),
            _resident((1, d)), _resident((1, d)), _resident((1, d)), _resident((d, d)), _resident((1, d))]


def _conv_prompt(x, weights, *, tile_t):
    b, t, d = x.shape
    tt = min(tile_t, t)
    assert t % tt == 0 and tt % 64 == 0 and d % 256 == 0
    kern = functools.partial(_conv_prompt_kernel, rows=64, cols=256)
    return pl.pallas_call(
        kern,
        grid=(b, t // tt),
        in_specs=[pl.BlockSpec((None, tt, d), lambda i, j: (i, j, 0))] + _conv_weight_specs(d),
        out_specs=[pl.BlockSpec((None, tt, d), lambda i, j: (i, j, 0)),
                   pl.BlockSpec((None, CONV_STATE, d), lambda i, j: (i, 0, 0))],
        out_shape=[jax.ShapeDtypeStruct((b, t, d), F32), jax.ShapeDtypeStruct((b, CONV_STATE, d), F32)],
        scratch_shapes=[pltpu.VMEM((HALO + tt, d), F32), pltpu.VMEM((tt, d), F32)],
        compiler_params=pltpu.CompilerParams(dimension_semantics=("arbitrary", "arbitrary"),
                                             vmem_limit_bytes=VMEM_LIMIT),
        name="conv_prompt",
    )(x, *weights)


def _conv_sample(x, cache, weights):
    nb, ts, d = x.shape
    assert ts == SUBLANES and cache.shape == (nb, CONV_STATE, d)
    n = nb * ts
    kern = functools.partial(_conv_sample_kernel, cols=128)
    y, st = pl.pallas_call(
        kern,
        grid=(1,),
        in_specs=[_resident((n, d)), _resident((nb, CONV_STATE, d))] + _conv_weight_specs(d),
        out_specs=[pl.BlockSpec((n, d), lambda i: (0, 0)), pl.BlockSpec((nb, CONV_STATE, d), lambda i: (0, 0, 0))],
        out_shape=[jax.ShapeDtypeStruct((n, d), F32), jax.ShapeDtypeStruct((nb, CONV_STATE, d), F32)],
        scratch_shapes=[pltpu.VMEM((nb, HALO + ts, d), F32), pltpu.VMEM((n, d), F32)],
        compiler_params=pltpu.CompilerParams(dimension_semantics=("arbitrary",), vmem_limit_bytes=VMEM_LIMIT),
        name="conv_sample",
    )(x.reshape(n, d), cache, *weights)
    return y.reshape(nb, ts, d), st


def _ffn_kernel(*refs, has_proj, has_final, ff_chunk):
    refs = list(refs)
    x_ref = refs.pop(0)
    if has_proj:
        o_ref, wo_ref = refs.pop(0), refs.pop(0)
    g_ref, wg_ref, wu_ref, wd_ref = refs.pop(0), refs.pop(0), refs.pop(0), refs.pop(0)
    if has_final:
        gf_ref = refs.pop(0)
    y_ref, h_ref = refs
    x = x_ref[...]
    if has_proj:
        x = x + _dot(o_ref[...].astype(BF16), wo_ref[...])
    h_ref[...] = _rmsnorm(x, g_ref[...]).astype(BF16)
    acc = x
    d_ff = wg_ref.shape[1]
    for c0 in range(0, d_ff, ff_chunk):
        h = h_ref[...]
        gate = _dot(h, wg_ref[:, c0:c0 + ff_chunk])
        up = _dot(h, wu_ref[:, c0:c0 + ff_chunk])
        act = (gate * _sigmoid(gate) * up).astype(BF16)
        acc = acc + _dot(act, wd_ref[c0:c0 + ff_chunk, :])
    if has_final:
        acc = _rmsnorm(acc, gf_ref[...])
    y_ref[...] = acc


def _ffn(x, g, wg, wu, wd, *, tile_m, proj=None, final_g=None):
    n, d = x.shape
    d_ff = wg.shape[1]
    tm = min(tile_m, n)
    assert n % tm == 0 and d_ff % MXU_DIM == 0
    tok = lambda last, dt=None: pl.BlockSpec((tm, last), lambda i: (i, 0))
    args, specs = [x], [tok(d)]
    if proj is not None:
        o, wo = proj
        args += [o, wo.astype(BF16)]
        specs += [tok(o.shape[1]), _resident(wo.shape)]
    args += [g.reshape(1, d), wg.astype(BF16), wu.astype(BF16), wd.astype(BF16)]
    specs += [_resident((1, d)), _resident((d, d_ff)), _resident((d, d_ff)), _resident((d_ff, d))]
    if final_g is not None:
        args.append(final_g.reshape(1, d))
        specs.append(_resident((1, d)))
    kern = functools.partial(_ffn_kernel, has_proj=proj is not None, has_final=final_g is not None, ff_chunk=MXU_DIM)
    return pl.pallas_call(
        kern,
        grid=(n // tm,),
        in_specs=specs,
        out_specs=tok(d),
        out_shape=jax.ShapeDtypeStruct((n, d), F32),
        scratch_shapes=[pltpu.VMEM((tm, d), BF16)],
        compiler_params=pltpu.CompilerParams(dimension_semantics=("parallel",), vmem_limit_bytes=VMEM_LIMIT),
        name="ffn",
    )(*args)


def _qkv_kernel(x_ref, g_ref, w_ref, *rest, paged):
    tm, d = x_ref.shape
    h = _rmsnorm(x_ref[...], g_ref[...]).astype(BF16)
    q = _dot(h, w_ref[:, :d]) * (HEAD_DIM ** -0.5)
    k = _dot(h, w_ref[:, d:2 * d])
    v = _dot(h, w_ref[:, 2 * d:])
    if paged:
        wt_ref, q_ref, kb_ref, vb_ref, kt_ref, vt_ref = rest
        kb_ref[...] = k.astype(BF16)
        vb_ref[...] = v.astype(BF16)
        for t_ref, r0 in ((kt_ref, 0), (vt_ref, d)):
            xt = _dot_nt(wt_ref[r0:r0 + d, :], h)
            for p in range(tm // PAGE_SIZE):
                t_ref[p] = xt[:, p * PAGE_SIZE:(p + 1) * PAGE_SIZE]
    else:
        q_ref, k_ref, v_ref = rest
        k_ref[...] = k
        v_ref[...] = v
    q_ref[...] = q.astype(q_ref.dtype)


def _qkv(x, g, w_qkv, *, tile_m, paged):
    n, d = x.shape
    assert w_qkv.shape == (d, 3 * d) and d == N_HEADS * HEAD_DIM
    tm = min(tile_m, n)
    assert n % tm == 0
    tok = pl.BlockSpec((tm, d), lambda i: (i, 0))
    f32o, bf16o = jax.ShapeDtypeStruct((n, d), F32), jax.ShapeDtypeStruct((n, d), BF16)
    args = [x, g.reshape(1, d), w_qkv.astype(BF16)]
    in_specs = [tok, _resident((1, d)), _resident((d, 3 * d))]
    if paged:
        assert tm % PAGE_SIZE == 0
        pages = pl.BlockSpec((tm // PAGE_SIZE, d, PAGE_SIZE), lambda i: (i, 0, 0))
        paged_o = jax.ShapeDtypeStruct((n // PAGE_SIZE, d, PAGE_SIZE), F32)
        args.append(jnp.transpose(w_qkv[:, d:]).astype(BF16))
        in_specs.append(_resident((2 * d, d)))
        out_shape, out_specs = [bf16o, bf16o, bf16o, paged_o, paged_o], [tok, tok, tok, pages, pages]
    else:
        out_shape, out_specs = [f32o, f32o, f32o], [tok, tok, tok]
    return pl.pallas_call(
        functools.partial(_qkv_kernel, paged=paged),
        grid=(n // tm,),
        in_specs=in_specs,
        out_specs=out_specs,
        out_shape=out_shape,
        compiler_params=pltpu.CompilerParams(dimension_semantics=("parallel",), vmem_limit_bytes=VMEM_LIMIT),
        name="qkv",
    )(*args)


def _sb_prompt_kernel(bias_ref, q_ref, k_ref, v_ref, o_ref,
                      qq_ref, tri_ref, u_ref, hl_ref, rs_ref, r_ref, acc_ref, *, tq, tk):
    hp, qi = pl.program_id(1), pl.program_id(2)
    sub = tq // tk
    lane = lax.broadcasted_iota(jnp.int32, (1, LANES), 1)
    q = q_ref[...]
    zero = jnp.zeros_like(q)
    qq_ref[:tq] = jnp.where(lane < HEAD_DIM, q, zero)
    qq_ref[tq:] = jnp.where(lane >= HEAD_DIM, q, zero)
    tri = (lax.broadcasted_iota(jnp.int32, (tk, tk), 0) > lax.broadcasted_iota(jnp.int32, (tk, tk), 1)).astype(BF16)
    tri_ref[:tk] = tri
    tri_ref[tk:] = tri
    r_ref[...] = jnp.zeros_like(r_ref)
    acc_ref[...] = jnp.zeros_like(acc_ref)
    bias = (bias_ref[2 * hp], bias_ref[2 * hp + 1])

    def stage_a(kb, slot, masked):
        start = pl.multiple_of(kb * tk, tk)
        z = _dot_nt(qq_ref[...], k_ref[pl.ds(start, tk), :])
        if masked:
            q_pos = qi * tq + lax.broadcasted_iota(jnp.int32, (tq, tk), 0)
            k_pos = kb * tk + lax.broadcasted_iota(jnp.int32, (tq, tk), 1)
            causal = k_pos < q_pos
        for hh in range(2):
            rows = slice(hh * tq, (hh + 1) * tq)
            zh = z[rows] + bias[hh]
            sp = jnp.maximum(zh, 0.0) + jnp.log(1.0 + jnp.exp2(jnp.abs(zh) * -LOG2_E))
            u = zh - sp
            if masked:
                sp = jnp.where(causal, sp, 0.0)
                u = jnp.where(causal, u, -jnp.inf)
            hi, lo = _split_bf16(sp)
            u_ref[slot, rows] = u
            hl_ref[slot, rows, :tk] = hi
            hl_ref[slot, rows, tk:] = lo
            rs_ref[slot, rows] = jnp.broadcast_to(jnp.sum(sp, axis=-1, keepdims=True), (tq, LANES))

    def stage_b(kb, slot):
        start = pl.multiple_of(kb * tk, tk)
        after = _dot(hl_ref[slot], tri_ref[...])
        r = r_ref[...]
        e = u_ref[slot] - after - jnp.concatenate([r] * (tk // LANES), axis=1)
        acc_ref[...] += _dot(jnp.exp(e).astype(BF16), v_ref[pl.ds(start, tk), :])
        r_ref[...] = r + rs_ref[slot]

    top = (qi + 1) * sub - 1
    stage_a(top, 0, True)
    for i in range(1, sub):
        stage_a(top - i, i % 2, True)
        stage_b(top - i + 1, (i - 1) % 2)

    def pair(t, carry):
        kb = qi * sub - 1 - 2 * t
        stage_a(kb, 0, False)
        stage_b(kb + 1, 1)
        stage_a(kb - 1, 1, False)
        stage_b(kb, 0)
        return carry

    lax.fori_loop(0, qi * (sub // 2), pair, 0)
    stage_b(0, 1)
    o_ref[...] = jnp.where(lane < HEAD_DIM, acc_ref[:tq], acc_ref[tq:]).astype(o_ref.dtype)


def _sb_prompt(q, k, v, bias, *, tq, tk):
    b, t, d = q.shape
    assert t % tq == 0 and tq % (2 * tk) == 0 and tk % LANES == 0
    assert d == N_HEADS * HEAD_DIM and 2 * HEAD_DIM == LANES
    kv_spec = pl.BlockSpec((None, t, LANES), lambda i, h, j: (i, 0, h))
    q_spec = pl.BlockSpec((None, tq, LANES), lambda i, h, j: (i, j, h))
    m = 2 * tq
    return pl.pallas_call(
        functools.partial(_sb_prompt_kernel, tq=tq, tk=tk),
        grid=(b, N_HEADS // 2, t // tq),
        in_specs=[pl.BlockSpec(memory_space=pltpu.SMEM), q_spec, kv_spec, kv_spec],
        out_specs=q_spec,
        out_shape=jax.ShapeDtypeStruct((b, t, d), BF16),
        scratch_shapes=[pltpu.VMEM((m, LANES), BF16), pltpu.VMEM((2 * tk, tk), BF16),
                        pltpu.VMEM((2, m, tk), F32), pltpu.VMEM((2, m, 2 * tk), BF16),
                        pltpu.VMEM((2, m, LANES), F32), pltpu.VMEM((m, LANES), F32), pltpu.VMEM((m, LANES), F32)],
        compiler_params=pltpu.CompilerParams(dimension_semantics=("parallel", "parallel", "arbitrary"),
                                             vmem_limit_bytes=VMEM_LIMIT),
        name="sb_prompt",
    )(bias.astype(F32), q, k, v)


def _sb_sample_kernel(pt_ref, bias_ref, q_ref, kn_ref, vn_ref, kpool_ref, vpool_ref, o_ref,
                      kbuf_ref, vbuf_ref, pad_ref, qbd_ref, r_ref, acc_ref, sem_ref, *, n_chunks, pages):
    n = pl.program_id(0)
    n_steps = pl.num_programs(0)
    ts, d = q_ref.shape
    hq = N_HEADS * ts
    n_pages = n_chunks * pages
    group = MXU_DIM // PAGE_SIZE
    blk = group * PAGE_SIZE

    def copies(step, slot):
        seq = step // n_chunks
        first = (n_chunks - 1 - step % n_chunks) * pages
        out = []
        for p in range(pages):
            page = pt_ref[seq * n_pages + first + p]
            out.append(pltpu.make_async_copy(kpool_ref.at[page], kbuf_ref.at[slot, p], sem_ref.at[0, slot]))
            out.append(pltpu.make_async_copy(vpool_ref.at[page], vbuf_ref.at[slot, p], sem_ref.at[1, slot]))
        return out

    slot = n % 2

    @pl.when(n == 0)
    def _():
        for cp in copies(n, slot):
            cp.start()

    @pl.when(n + 1 < n_steps)
    def _():
        for cp in copies(n + 1, 1 - slot):
            cp.start()

    row = lax.broadcasted_iota(jnp.int32, (blk, blk), 0)
    col = lax.broadcasted_iota(jnp.int32, (blk, blk), 1)
    tri = (row > col).astype(BF16)

    def fold(z, valid, times_v):
        m = z.shape[1]
        lg = _neg_softplus(z)
        if valid is not None:
            lg = jnp.where(valid, lg, 0.0)
        hi, lo = _split_bf16(lg)
        t = tri[:m, :m]
        a = jnp.exp(z + lg + _dot(hi, t) + _dot(lo, t) + r_ref[...])
        if valid is not None:
            a = jnp.where(valid, a, 0.0)
        acc_ref[...] += times_v(a.astype(BF16))
        r_ref[...] += jnp.sum(lg, axis=-1, keepdims=True)

    @pl.when(n % n_chunks == 0)
    def _():
        q = jnp.concatenate([q_ref[...]] * N_HEADS, axis=0)
        qr = lax.broadcasted_iota(jnp.int32, (hq, d), 0)
        qc = lax.broadcasted_iota(jnp.int32, (hq, d), 1)
        qbd_ref[...] = jnp.where(qr // ts == qc // HEAD_DIM, q, 0.0).astype(BF16)
        r_ref[...] = jnp.zeros_like(r_ref)
        acc_ref[...] = jnp.zeros_like(acc_ref)
        pad_ref[...] = jnp.zeros_like(pad_ref)
        pad_ref[0, 0:ts, :] = kn_ref[...]
        pad_ref[1, 0:ts, :] = vn_ref[...]
        qry = lax.broadcasted_iota(jnp.int32, (hq, PAGE_SIZE), 0) % ts
        key = lax.broadcasted_iota(jnp.int32, (hq, PAGE_SIZE), 1)
        z = _dot_nt(qbd_ref[...], pad_ref[0].astype(BF16)) + bias_ref[...]
        fold(z, key < qry, lambda a: _dot(a, pad_ref[1].astype(BF16)))

    for cp in copies(n, slot):
        cp.wait()
    for g in reversed(range(pages // group)):
        kt = jnp.concatenate([kbuf_ref[slot, g * group + i] for i in range(group)], axis=1).astype(BF16)
        z = _dot(qbd_ref[...], kt) + bias_ref[...]

        def times_v(a, g=g):
            vt = jnp.concatenate([vbuf_ref[slot, g * group + i] for i in range(group)], axis=1).astype(BF16)
            return _dot_nt(a, vt)

        fold(z, None, times_v)

    @pl.when(n % n_chunks == n_chunks - 1)
    def _():
        orow = lax.broadcasted_iota(jnp.int32, (hq, d), 0)
        ocol = lax.broadcasted_iota(jnp.int32, (hq, d), 1)
        o = jnp.where(orow // ts == ocol // HEAD_DIM, acc_ref[...], 0.0)
        o_ref[...] = jnp.sum(o.reshape(N_HEADS, ts, d), axis=0)


def _sb_sample(q, k_new, v_new, k_pool, v_pool, page_table, bias, *, pages):
    nb, ts, d = q.shape
    n_pages = page_table.shape[1]
    pages = min(pages, n_pages)
    group = MXU_DIM // PAGE_SIZE
    assert ts == SUBLANES and n_pages % pages == 0 and pages % group == 0 and N_HEADS * ts == LANES
    assert k_pool.shape[1:] == (PAGE_SIZE, N_HEADS, HEAD_DIM)
    n_chunks = n_pages // pages
    n_pool = k_pool.shape[0]
    hq = N_HEADS * ts
    pool_t = lambda a: jnp.transpose(a, (0, 2, 3, 1)).reshape(n_pool, d, PAGE_SIZE)
    tok = pl.BlockSpec((None, ts, d), lambda i, pt: (i // n_chunks, 0, 0))
    grid_spec = pltpu.PrefetchScalarGridSpec(
        num_scalar_prefetch=1,
        grid=(nb * n_chunks,),
        in_specs=[pl.BlockSpec((hq, 1), lambda i, pt: (0, 0)), tok, tok, tok,
                  pl.BlockSpec(memory_space=pl.ANY), pl.BlockSpec(memory_space=pl.ANY)],
        out_specs=tok,
        scratch_shapes=[pltpu.VMEM((2, pages, d, PAGE_SIZE), F32), pltpu.VMEM((2, pages, d, PAGE_SIZE), F32),
                        pltpu.VMEM((2, PAGE_SIZE, d), F32), pltpu.VMEM((hq, d), BF16),
                        pltpu.VMEM((hq, 1), F32), pltpu.VMEM((hq, d), F32),
                        pltpu.SemaphoreType.DMA((2, 2))],
    )
    return pl.pallas_call(
        functools.partial(_sb_sample_kernel, n_chunks=n_chunks, pages=pages),
        grid_spec=grid_spec,
        out_shape=jax.ShapeDtypeStruct((nb, ts, d), F32),
        compiler_params=pltpu.CompilerParams(dimension_semantics=("arbitrary",), vmem_limit_bytes=VMEM_LIMIT),
        name="sb_sample",
    )(page_table.reshape(-1), jnp.repeat(bias.astype(F32), ts).reshape(hq, 1), q, k_new, v_new,
      pool_t(k_pool), pool_t(v_pool))


def kernel(x_prompt, x_sample, cache_conv, cache_k, cache_v, page_table, mix_norm_g, ffn_norm_g, final_norm_g,
           cv_w_pw1, cv_b_pw1, cv_w_dw, cv_b_dw, cv_ln_g, cv_ln_b, cv_w_pw2, cv_b_pw2,
           sb_w_qkv, sb_w_o, sb_logit_bias, ffn_w_gate, ffn_w_up, ffn_w_down):
    bp, tp, d = x_prompt.shape
    bs, ts, _ = x_sample.shape
    assert mix_norm_g.shape[0] == 2 and tp % PAGE_SIZE == 0

    cw = _conv_weights(mix_norm_g[0], cv_w_pw1[0], cv_b_pw1[0], cv_w_dw[0], cv_b_dw[0], cv_ln_g[0], cv_ln_b[0],
                       cv_w_pw2[0], cv_b_pw2[0])
    xp, st_p = _conv_prompt(x_prompt, cw, tile_t=256)
    xs, st_s = _conv_sample(x_sample, cache_conv[0], cw)
    ffn0 = (ffn_norm_g[0], ffn_w_gate[0], ffn_w_up[0], ffn_w_down[0])
    xp = _ffn(xp.reshape(bp * tp, d), *ffn0, tile_m=512)
    xs = _ffn(xs.reshape(bs * ts, d), *ffn0, tile_m=512)

    qp, kpb, vpb, kp_t, vp_t = _qkv(xp, mix_norm_g[1], sb_w_qkv[0], tile_m=512, paged=True)
    qs, ks, vs = _qkv(xs, mix_norm_g[1], sb_w_qkv[0], tile_m=512, paged=False)
    seq = lambda a: a.reshape(bp, tp, d)
    op = _sb_prompt(seq(qp), seq(kpb), seq(vpb), sb_logit_bias[0], tq=512, tk=256)
    dec = lambda a: a.reshape(bs, ts, d)
    os_ = _sb_sample(dec(qs), dec(ks), dec(vs), cache_k[0], cache_v[0], page_table, sb_logit_bias[0], pages=8)
    ffn1 = (ffn_norm_g[1], ffn_w_gate[1], ffn_w_up[1], ffn_w_down[1])
    yp = _ffn(xp, *ffn1, tile_m=512, proj=(op.reshape(bp * tp, d), sb_w_o[0]), final_g=final_norm_g)
    ys = _ffn(xs, *ffn1, tile_m=512, proj=(os_.reshape(bs * ts, d), sb_w_o[0]), final_g=final_norm_g)

    pages_p = lambda a: jnp.transpose(a.reshape(1, bp, tp // PAGE_SIZE, N_HEADS, HEAD_DIM, PAGE_SIZE), (0, 1, 2, 5, 3, 4))
    new_s = (1, bs, ts, N_HEADS, HEAD_DIM)
    return (yp.reshape(bp, tp, d), ys.reshape(bs, ts, d), st_p[None], st_s[None],
            pages_p(kp_t), pages_p(vp_t), ks.reshape(new_s), vs.reshape(new_s))
```

```python
import functools

import jax
import jax.numpy as jnp
from jax import lax
from jax.experimental import pallas as pl
from jax.experimental.pallas import tpu as pltpu

N_HEADS = 16
HEAD_DIM = 64
CONV_WIDTH = 31
CONV_STATE = CONV_WIDTH - 1
PAGE_SIZE = 128
RMS_EPS = 1e-6
LN_EPS = 1e-5
LOG2_E = 1.4426950408889634

F32 = jnp.float32
BF16 = jnp.bfloat16

LANES = 128
SUBLANES = 8
MXU_DIM = 256
HALO = 32
VMEM_LIMIT = 56 * 1024 * 1024


def _resident(shape):
    nd = len(shape)
    return pl.BlockSpec(shape, lambda *_: (0,) * nd, pipeline_mode=pl.Buffered(1))


def _rmsnorm(x, g):
    return x * lax.rsqrt(jnp.mean(x * x, axis=-1, keepdims=True) + RMS_EPS) * g


def _sigmoid(x):
    return 1.0 / (1.0 + jnp.exp(-x))


def _dot(a, b):
    return jnp.dot(a, b, preferred_element_type=F32)


def _dot_nt(a, b):
    return lax.dot_general(a, b, (((1,), (1,)), ((), ())), preferred_element_type=F32)


def _softplus(z):
    return jnp.maximum(z, 0.0) + jnp.log(1.0 + jnp.exp2(jnp.abs(z) * -LOG2_E))


def _glu(h, w1_ref, b1_ref, d):
    a = _dot(h, w1_ref[:, :d]) + b1_ref[:, :d]
    gate = _dot(h, w1_ref[:, d:]) + b1_ref[:, d:]
    return a * _sigmoid(gate)


def _ln_silu_pw2(c, lng_ref, lnb_ref, w2_ref, b2_ref):
    mu = jnp.mean(c, axis=-1, keepdims=True)
    cc = c - mu
    var = jnp.mean(cc * cc, axis=-1, keepdims=True)
    y = cc * lax.rsqrt(var + LN_EPS) * lng_ref[...] + lnb_ref[...]
    y = y * _sigmoid(y)
    return _dot(y.astype(BF16), w2_ref[...]) + b2_ref[...]


def _conv_prompt_kernel(x_ref, g_ref, w1_ref, b1_ref, wdw_ref, bdw_ref, lng_ref, lnb_ref, w2_ref, b2_ref,
                        y_ref, st_ref, ubuf_ref, cbuf_ref, *, rows, cols):
    tt, d = x_ref.shape

    @pl.when(pl.program_id(1) == 0)
    def _():
        ubuf_ref[0:HALO, :] = jnp.zeros((HALO, d), F32)

    x = x_ref[...]
    h = _rmsnorm(x, g_ref[...]).astype(BF16)
    ubuf_ref[HALO:HALO + tt, :] = _glu(h, w1_ref, b1_ref, d)

    off = HALO - CONV_STATE
    for c0 in range(0, d, cols):
        for r0 in range(0, tt, rows):
            acc = jnp.broadcast_to(bdw_ref[:, c0:c0 + cols], (rows, cols))
            for phase in range(SUBLANES):
                taps = [m for m in range(phase, HALO + 1, SUBLANES) if 0 <= m - off < CONV_WIDTH]
                if not taps:
                    continue
                slab = ubuf_ref[r0 + phase:r0 + phase + rows + taps[-1] - phase, c0:c0 + cols]
                for m in taps:
                    w = wdw_ref[m - off, :, c0:c0 + cols]
                    seg = slab[m - phase:m - phase + rows].reshape(rows // SUBLANES, SUBLANES, cols)
                    acc = acc + (seg * w[None]).reshape(rows, cols)
            cbuf_ref[r0:r0 + rows, c0:c0 + cols] = acc

    y_ref[...] = x + _ln_silu_pw2(cbuf_ref[...], lng_ref, lnb_ref, w2_ref, b2_ref)
    st_ref[...] = ubuf_ref[tt + off:tt + HALO, :]
    ubuf_ref[0:HALO, :] = ubuf_ref[tt:tt + HALO, :]


def _conv_sample_kernel(x_ref, cache_ref, g_ref, w1_ref, b1_ref, wdw_ref, bdw_ref, lng_ref, lnb_ref, w2_ref, b2_ref,
                        y_ref, st_ref, ext_ref, cbuf_ref, *, cols):
    n, d = x_ref.shape
    nb, ts = ext_ref.shape[0], ext_ref.shape[1] - HALO
    off = HALO - CONV_STATE
    x = x_ref[...]
    h = _rmsnorm(x, g_ref[...]).astype(BF16)
    ext_ref[:, off:HALO, :] = cache_ref[...]
    ext_ref[:, HALO:, :] = _glu(h, w1_ref, b1_ref, d).reshape(nb, ts, d)
    for c0 in range(0, d, cols):
        acc = jnp.broadcast_to(bdw_ref[:, c0:c0 + cols][None], (nb, ts, cols))
        for k in range(CONV_WIDTH):
            acc = acc + ext_ref[:, off + k:off + k + ts, c0:c0 + cols] * wdw_ref[k, :, c0:c0 + cols][None]
        cbuf_ref[:, c0:c0 + cols] = acc.reshape(n, cols)
    y_ref[...] = x + _ln_silu_pw2(cbuf_ref[...], lng_ref, lnb_ref, w2_ref, b2_ref)
    st_ref[...] = ext_ref[:, ts + off:, :]


def _conv_weights(g, w1, b1, wdw, bdw, lng, lnb, w2, b2):
    d = w2.shape[0]
    row = lambda v: v.reshape(1, -1).astype(F32)
    wdw8 = jnp.broadcast_to(wdw[:, None, :], (CONV_WIDTH, SUBLANES, d))
    return (row(g), w1.astype(BF16), row(b1), wdw8, row(bdw), row(lng), row(lnb), w2.astype(BF16), row(b2))


def _conv_weight_specs(d):
    return [_resident((1, d)), _resident((d, 2 * d)), _resident((1, 2 * d)), _resident((CONV_WIDTH, SUBLANES, d)),
            _resident((1, d)), _resident((1, d)), _resident((1, d)), _resident((d, d)), _resident((1, d))]


def _conv_prompt(x, weights, *, tile_t):
    b, t, d = x.shape
    tt = min(tile_t, t)
    assert t % tt == 0 and tt % 64 == 0 and d % 256 == 0
    kern = functools.partial(_conv_prompt_kernel, rows=64, cols=256)
    return pl.pallas_call(
        kern,
        grid=(b, t // tt),
        in_specs=[pl.BlockSpec((None, tt, d), lambda i, j: (i, j, 0))] + _conv_weight_specs(d),
        out_specs=[pl.BlockSpec((None, tt, d), lambda i, j: (i, j, 0)),
                   pl.BlockSpec((None, CONV_STATE, d), lambda i, j: (i, 0, 0))],
        out_shape=[jax.ShapeDtypeStruct((b, t, d), F32), jax.ShapeDtypeStruct((b, CONV_STATE, d), F32)],
        scratch_shapes=[pltpu.VMEM((HALO + tt, d), F32), pltpu.VMEM((tt, d), F32)],
        compiler_params=pltpu.CompilerParams(dimension_semantics=("arbitrary", "arbitrary"),
                                             vmem_limit_bytes=VMEM_LIMIT),
        name="conv_prompt",
    )(x, *weights)


def _conv_sample(x, cache, weights):
    nb, ts, d = x.shape
    assert ts == SUBLANES and cache.shape == (nb, CONV_STATE, d)
    n = nb * ts
    kern = functools.partial(_conv_sample_kernel, cols=128)
    y, st = pl.pallas_call(
        kern,
        grid=(1,),
        in_specs=[_resident((n, d)), _resident((nb, CONV_STATE, d))] + _conv_weight_specs(d),
        out_specs=[pl.BlockSpec((n, d), lambda i: (0, 0)), pl.BlockSpec((nb, CONV_STATE, d), lambda i: (0, 0, 0))],
        out_shape=[jax.ShapeDtypeStruct((n, d), F32), jax.ShapeDtypeStruct((nb, CONV_STATE, d), F32)],
        scratch_shapes=[pltpu.VMEM((nb, HALO + ts, d), F32), pltpu.VMEM((n, d), F32)],
        compiler_params=pltpu.CompilerParams(dimension_semantics=("arbitrary",), vmem_limit_bytes=VMEM_LIMIT),
        name="conv_sample",
    )(x.reshape(n, d), cache, *weights)
    return y.reshape(nb, ts, d), st


def _ffn_kernel(*refs, has_proj, has_final, ff_chunk):
    refs = list(refs)
    x_ref = refs.pop(0)
    if has_proj:
        o_ref, wo_ref = refs.pop(0), refs.pop(0)
    g_ref, wg_ref, wu_ref, wd_ref = refs.pop(0), refs.pop(0), refs.pop(0), refs.pop(0)
    if has_final:
        gf_ref = refs.pop(0)
    y_ref, h_ref = refs
    x = x_ref[...]
    if has_proj:
        x = x + _dot(o_ref[...].astype(BF16), wo_ref[...])
    h_ref[...] = _rmsnorm(x, g_ref[...]).astype(BF16)
    acc = x
    d_ff = wg_ref.shape[1]
    for c0 in range(0, d_ff, ff_chunk):
        h = h_ref[...]
        gate = _dot(h, wg_ref[:, c0:c0 + ff_chunk])
        up = _dot(h, wu_ref[:, c0:c0 + ff_chunk])
        act = (gate * _sigmoid(gate) * up).astype(BF16)
        acc = acc + _dot(act, wd_ref[c0:c0 + ff_chunk, :])
    if has_final:
        acc = _rmsnorm(acc, gf_ref[...])
    y_ref[...] = acc


def _ffn(x, g, wg, wu, wd, *, tile_m, proj=None, final_g=None):
    n, d = x.shape
    d_ff = wg.shape[1]
    tm = min(tile_m, n)
    assert n % tm == 0 and d_ff % MXU_DIM == 0
    tok = lambda last, dt=None: pl.BlockSpec((tm, last), lambda i: (i, 0))
    args, specs = [x], [tok(d)]
    if proj is not None:
        o, wo = proj
        args += [o, wo.astype(BF16)]
        specs += [tok(o.shape[1]), _resident(wo.shape)]
    args += [g.reshape(1, d), wg.astype(BF16), wu.astype(BF16), wd.astype(BF16)]
    specs += [_resident((1, d)), _resident((d, d_ff)), _resident((d, d_ff)), _resident((d_ff, d))]
    if final_g is not None:
        args.append(final_g.reshape(1, d))
        specs.append(_resident((1, d)))
    kern = functools.partial(_ffn_kernel, has_proj=proj is not None, has_final=final_g is not None, ff_chunk=MXU_DIM)
    return pl.pallas_call(
        kern,
        grid=(n // tm,),
        in_specs=specs,
        out_specs=tok(d),
        out_shape=jax.ShapeDtypeStruct((n, d), F32),
        scratch_shapes=[pltpu.VMEM((tm, d), BF16)],
        compiler_params=pltpu.CompilerParams(dimension_semantics=("parallel",), vmem_limit_bytes=VMEM_LIMIT),
        name="ffn",
    )(*args)


def _qkv_kernel(x_ref, g_ref, w_ref, *rest, paged):
    tm, d = x_ref.shape
    h = _rmsnorm(x_ref[...], g_ref[...]).astype(BF16)
    q = _dot(h, w_ref[:, :d]) * (HEAD_DIM ** -0.5)
    k = _dot(h, w_ref[:, d:2 * d])
    v = _dot(h, w_ref[:, 2 * d:])
    if paged:
        wt_ref, q_ref, kb_ref, vb_ref, kt_ref, vt_ref = rest
        kb_ref[...] = k.astype(BF16)
        vb_ref[...] = v.astype(BF16)
        for t_ref, r0 in ((kt_ref, 0), (vt_ref, d)):
            xt = _dot_nt(wt_ref[r0:r0 + d, :], h)
            for p in range(tm // PAGE_SIZE):
                t_ref[p] = xt[:, p * PAGE_SIZE:(p + 1) * PAGE_SIZE]
    else:
        q_ref, k_ref, v_ref = rest
        k_ref[...] = k
        v_ref[...] = v
    q_ref[...] = q.astype(q_ref.dtype)


def _qkv(x, g, w_qkv, *, tile_m, paged):
    n, d = x.shape
    assert w_qkv.shape == (d, 3 * d) and d == N_HEADS * HEAD_DIM
    tm = min(tile_m, n)
    assert n % tm == 0
    tok = pl.BlockSpec((tm, d), lambda i: (i, 0))
    f32o, bf16o = jax.ShapeDtypeStruct((n, d), F32), jax.ShapeDtypeStruct((n, d), BF16)
    args = [x, g.reshape(1, d), w_qkv.astype(BF16)]
    in_specs = [tok, _resident((1, d)), _resident((d, 3 * d))]
    if paged:
        assert tm % PAGE_SIZE == 0
        pages = pl.BlockSpec((tm // PAGE_SIZE, d, PAGE_SIZE), lambda i: (i, 0, 0))
        paged_o = jax.ShapeDtypeStruct((n // PAGE_SIZE, d, PAGE_SIZE), F32)
        args.append(jnp.transpose(w_qkv[:, d:]).astype(BF16))
        in_specs.append(_resident((2 * d, d)))
        out_shape, out_specs = [bf16o, bf16o, bf16o, paged_o, paged_o], [tok, tok, tok, pages, pages]
    else:
        out_shape, out_specs = [f32o, f32o, f32o], [tok, tok, tok]
    return pl.pallas_call(
        functools.partial(_qkv_kernel, paged=paged),
        grid=(n // tm,),
        in_specs=in_specs,
        out_specs=out_specs,
        out_shape=out_shape,
        compiler_params=pltpu.CompilerParams(dimension_semantics=("parallel",), vmem_limit_bytes=VMEM_LIMIT),
        name="qkv",
    )(*args)


def _sb_prompt_kernel(bias_ref, q_ref, k_ref, v_ref, o_ref,
                      qq_ref, tri_ref, u_ref, h_ref, rs_ref, r_ref, acc_ref, *, tq, tk):
    hp, qi = pl.program_id(1), pl.program_id(2)
    lane = lax.broadcasted_iota(jnp.int32, (1, LANES), 1)
    zero = jnp.zeros((tk, LANES), BF16)
    for c in range(4):
        qc = q_ref[(c // 2) * tk:(c // 2 + 1) * tk, :]
        qq_ref[c * tk:(c + 1) * tk] = jnp.where((lane >= HEAD_DIM) if c % 2 else (lane < HEAD_DIM), qc, zero)
    tri_ref[...] = (lax.broadcasted_iota(jnp.int32, (tk, tk), 0)
                    > lax.broadcasted_iota(jnp.int32, (tk, tk), 1)).astype(BF16)
    r_ref[...] = jnp.zeros_like(r_ref)
    acc_ref[...] = jnp.zeros_like(acc_ref)
    bias = (bias_ref[2 * hp], bias_ref[2 * hp + 1])

    def stage_a(kb, slot, masked, r0=0):
        start = pl.multiple_of(kb * tk, tk)
        z = _dot_nt(qq_ref[r0:], k_ref[pl.ds(start, tk), :])
        for c in range(r0 // tk, 4):
            rows = slice(c * tk, (c + 1) * tk)
            zc = z[c * tk - r0:(c + 1) * tk - r0] + bias[c % 2]
            sp = _softplus(zc)
            u = zc - sp
            if masked:
                q_pos = qi * tq + (c // 2) * tk + lax.broadcasted_iota(jnp.int32, (tk, tk), 0)
                k_pos = kb * tk + lax.broadcasted_iota(jnp.int32, (tk, tk), 1)
                causal = k_pos < q_pos
                sp = jnp.where(causal, sp, 0.0)
                u = jnp.where(causal, u, -jnp.inf)
            u_ref[slot, rows] = u
            h_ref[slot, rows] = sp.astype(BF16)
            rs_ref[slot, rows] = jnp.broadcast_to(jnp.sum(sp, axis=-1, keepdims=True), (tk, LANES))

    def stage_b(kb, slot, r0=0):
        start = pl.multiple_of(kb * tk, tk)
        after = _dot(h_ref[slot, r0:], tri_ref[...])
        r = r_ref[r0:]
        e = u_ref[slot, r0:] - after - jnp.concatenate([r] * (tk // LANES), axis=1)
        acc_ref[r0:] += _dot(jnp.exp(e).astype(BF16), v_ref[pl.ds(start, tk), :])
        r_ref[r0:] = r + rs_ref[slot, r0:]

    top = 2 * qi + 1
    stage_a(top, 0, True, r0=tq)
    stage_a(top - 1, 1, True)
    stage_b(top, 0, r0=tq)

    def pair(t, carry):
        kb = 2 * qi - 1 - 2 * t
        stage_a(kb, 0, False)
        stage_b(kb + 1, 1)
        stage_a(kb - 1, 1, False)
        stage_b(kb, 0)
        return carry

    lax.fori_loop(0, qi, pair, 0)
    stage_b(0, 1)
    for half in range(2):
        lo, hi = acc_ref[2 * half * tk:(2 * half + 1) * tk], acc_ref[(2 * half + 1) * tk:(2 * half + 2) * tk]
        o_ref[half * tk:(half + 1) * tk, :] = jnp.where(lane < HEAD_DIM, lo, hi).astype(o_ref.dtype)


def _sb_prompt(q, k, v, bias, *, tq, tk):
    b, t, d = q.shape
    assert t % tq == 0 and tq == 2 * tk and tk % LANES == 0
    assert d == N_HEADS * HEAD_DIM and 2 * HEAD_DIM == LANES
    kv_spec = pl.BlockSpec((None, t, LANES), lambda i, h, j: (i, 0, h))
    q_spec = pl.BlockSpec((None, tq, LANES), lambda i, h, j: (i, j, h))
    m = 2 * tq
    return pl.pallas_call(
        functools.partial(_sb_prompt_kernel, tq=tq, tk=tk),
        grid=(b, N_HEADS // 2, t // tq),
        in_specs=[pl.BlockSpec(memory_space=pltpu.SMEM), q_spec, kv_spec, kv_spec],
        out_specs=q_spec,
        out_shape=jax.ShapeDtypeStruct((b, t, d), BF16),
        scratch_shapes=[pltpu.VMEM((m, LANES), BF16), pltpu.VMEM((tk, tk), BF16),
                        pltpu.VMEM((2, m, tk), F32), pltpu.VMEM((2, m, tk), BF16),
                        pltpu.VMEM((2, m, LANES), F32), pltpu.VMEM((m, LANES), F32), pltpu.VMEM((m, LANES), F32)],
        compiler_params=pltpu.CompilerParams(dimension_semantics=("parallel", "parallel", "arbitrary"),
                                             vmem_limit_bytes=VMEM_LIMIT),
        name="sb_prompt",
    )(bias.astype(F32), q, k, v)


def _sb_sample_kernel(pt_ref, bias_ref, q_ref, kn_ref, vn_ref, kpool_ref, vpool_ref, o_ref,
                      kbuf_ref, vbuf_ref, pad_ref, qbd_ref, r_ref, acc_ref, sem_ref, *, n_chunks, pages):
    n = pl.program_id(0)
    n_steps = pl.num_programs(0)
    ts, d = q_ref.shape
    hq = N_HEADS * ts
    n_pages = n_chunks * pages
    blk = MXU_DIM

    def copies(step, slot):
        seq = step // n_chunks
        first = (n_chunks - 1 - step % n_chunks) * pages
        out = []
        for p in range(pages):
            page = pt_ref[seq * n_pages + first + p]
            out.append(pltpu.make_async_copy(kpool_ref.at[page], kbuf_ref.at[slot, p], sem_ref.at[0, slot]))
            out.append(pltpu.make_async_copy(vpool_ref.at[page], vbuf_ref.at[slot, p], sem_ref.at[1, slot]))
        return out

    slot = n % 2

    @pl.when(n == 0)
    def _():
        for cp in copies(n, slot):
            cp.start()

    @pl.when(n + 1 < n_steps)
    def _():
        for cp in copies(n + 1, 1 - slot):
            cp.start()

    row = lax.broadcasted_iota(jnp.int32, (blk, blk), 0)
    col = lax.broadcasted_iota(jnp.int32, (blk, blk), 1)
    tri = (row > col).astype(BF16)

    def fold(z, valid, times_v):
        w = z.shape[1]
        step = min(blk, w)
        sp = _softplus(z)
        u = z - sp
        if valid is not None:
            sp = jnp.where(valid, sp, 0.0)
            u = jnp.where(valid, u, -jnp.inf)
        r = r_ref[...]
        es = [None] * (w // step)
        for j in reversed(range(w // step)):
            spj = sp[:, j * step:(j + 1) * step]
            after = _dot(spj.astype(BF16), tri[:step, :step])
            es[j] = u[:, j * step:(j + 1) * step] - after - r
            r = r + jnp.sum(spj, axis=-1, keepdims=True)
        acc_ref[...] += times_v(jnp.exp(jnp.concatenate(es, axis=1)).astype(BF16))
        r_ref[...] = r

    @pl.when(n % n_chunks == 0)
    def _():
        q = jnp.concatenate([q_ref[...]] * N_HEADS, axis=0)
        qr = lax.broadcasted_iota(jnp.int32, (hq, d), 0)
        qc = lax.broadcasted_iota(jnp.int32, (hq, d), 1)
        qbd_ref[...] = jnp.where(qr // ts == qc // HEAD_DIM, q, 0.0).astype(BF16)
        r_ref[...] = jnp.zeros_like(r_ref)
        acc_ref[...] = jnp.zeros_like(acc_ref)
        pad_ref[...] = jnp.zeros_like(pad_ref)
        pad_ref[0, 0:ts, :] = kn_ref[...]
        pad_ref[1, 0:ts, :] = vn_ref[...]
        qry = lax.broadcasted_iota(jnp.int32, (hq, PAGE_SIZE), 0) % ts
        key = lax.broadcasted_iota(jnp.int32, (hq, PAGE_SIZE), 1)
        z = _dot_nt(qbd_ref[...], pad_ref[0].astype(BF16)) + bias_ref[...]
        fold(z, key < qry, lambda a: _dot(a, pad_ref[1].astype(BF16)))

    for cp in copies(n, slot):
        cp.wait()
    all_pages = lambda buf: jnp.concatenate([buf[slot, p] for p in range(pages)], axis=1).astype(BF16)
    z = _dot(qbd_ref[...], all_pages(kbuf_ref)) + bias_ref[...]
    fold(z, None, lambda a: _dot_nt(a, all_pages(vbuf_ref)))

    @pl.when(n % n_chunks == n_chunks - 1)
    def _():
        orow = lax.broadcasted_iota(jnp.int32, (hq, d), 0)
        ocol = lax.broadcasted_iota(jnp.int32, (hq, d), 1)
        o = jnp.where(orow // ts == ocol // HEAD_DIM, acc_ref[...], 0.0)
        o_ref[...] = jnp.sum(o.reshape(N_HEADS, ts, d), axis=0)


def _sb_sample(q, k_new, v_new, k_pool, v_pool, page_table, bias, *, pages):
    nb, ts, d = q.shape
    n_pages = page_table.shape[1]
    pages = min(pages, n_pages)
    group = MXU_DIM // PAGE_SIZE
    assert ts == SUBLANES and n_pages % pages == 0 and pages % group == 0 and N_HEADS * ts == LANES
    assert k_pool.shape[1:] == (PAGE_SIZE, N_HEADS, HEAD_DIM)
    n_chunks = n_pages // pages
    n_pool = k_pool.shape[0]
    hq = N_HEADS * ts
    pool_t = lambda a: jnp.transpose(a, (0, 2, 3, 1)).reshape(n_pool, d, PAGE_SIZE)
    tok = pl.BlockSpec((None, ts, d), lambda i, pt: (i // n_chunks, 0, 0))
    grid_spec = pltpu.PrefetchScalarGridSpec(
        num_scalar_prefetch=1,
        grid=(nb * n_chunks,),
        in_specs=[pl.BlockSpec((hq, 1), lambda i, pt: (0, 0)), tok, tok, tok,
                  pl.BlockSpec(memory_space=pl.ANY), pl.BlockSpec(memory_space=pl.ANY)],
        out_specs=tok,
        scratch_shapes=[pltpu.VMEM((2, pages, d, PAGE_SIZE), F32), pltpu.VMEM((2, pages, d, PAGE_SIZE), F32),
                        pltpu.VMEM((2, PAGE_SIZE, d), F32), pltpu.VMEM((hq, d), BF16),
                        pltpu.VMEM((hq, 1), F32), pltpu.VMEM((hq, d), F32),
                        pltpu.SemaphoreType.DMA((2, 2))],
    )
    return pl.pallas_call(
        functools.partial(_sb_sample_kernel, n_chunks=n_chunks, pages=pages),
        grid_spec=grid_spec,
        out_shape=jax.ShapeDtypeStruct((nb, ts, d), F32),
        compiler_params=pltpu.CompilerParams(dimension_semantics=("arbitrary",), vmem_limit_bytes=VMEM_LIMIT),
        name="sb_sample",
    )(page_table.reshape(-1), jnp.repeat(bias.astype(F32), ts).reshape(hq, 1), q, k_new, v_new,
      pool_t(k_pool), pool_t(v_pool))


def kernel(x_prompt, x_sample, cache_conv, cache_k, cache_v, page_table, mix_norm_g, ffn_norm_g, final_norm_g,
           cv_w_pw1, cv_b_pw1, cv_w_dw, cv_b_dw, cv_ln_g, cv_ln_b, cv_w_pw2, cv_b_pw2,
           sb_w_qkv, sb_w_o, sb_logit_bias, ffn_w_gate, ffn_w_up, ffn_w_down):
    bp, tp, d = x_prompt.shape
    bs, ts, _ = x_sample.shape
    assert mix_norm_g.shape[0] == 2 and tp % PAGE_SIZE == 0

    cw = _conv_weights(mix_norm_g[0], cv_w_pw1[0], cv_b_pw1[0], cv_w_dw[0], cv_b_dw[0], cv_ln_g[0], cv_ln_b[0],
                       cv_w_pw2[0], cv_b_pw2[0])
    xp, st_p = _conv_prompt(x_prompt, cw, tile_t=256)
    xs, st_s = _conv_sample(x_sample, cache_conv[0], cw)
    ffn0 = (ffn_norm_g[0], ffn_w_gate[0], ffn_w_up[0], ffn_w_down[0])
    xp = _ffn(xp.reshape(bp * tp, d), *ffn0, tile_m=512)
    xs = _ffn(xs.reshape(bs * ts, d), *ffn0, tile_m=512)

    qp, kpb, vpb, kp_t, vp_t = _qkv(xp, mix_norm_g[1], sb_w_qkv[0], tile_m=512, paged=True)
    qs, ks, vs = _qkv(xs, mix_norm_g[1], sb_w_qkv[0], tile_m=512, paged=False)
    seq = lambda a: a.reshape(bp, tp, d)
    op = _sb_prompt(seq(qp), seq(kpb), seq(vpb), sb_logit_bias[0], tq=512, tk=256)
    dec = lambda a: a.reshape(bs, ts, d)
    os_ = _sb_sample(dec(qs), dec(ks), dec(vs), cache_k[0], cache_v[0], page_table, sb_logit_bias[0], pages=8)
    ffn1 = (ffn_norm_g[1], ffn_w_gate[1], ffn_w_up[1], ffn_w_down[1])
    yp = _ffn(xp, *ffn1, tile_m=512, proj=(op.reshape(bp * tp, d), sb_w_o[0]), final_g=final_norm_g)
    ys = _ffn(xs, *ffn1, tile_m=512, proj=(os_.reshape(bs * ts, d), sb_w_o[0]), final_g=final_norm_g)

    pages_p = lambda a: jnp.transpose(a.reshape(1, bp, tp // PAGE_SIZE, N_HEADS, HEAD_DIM, PAGE_SIZE), (0, 1, 2, 5, 3, 4))
    new_s = (1, bs, ts, N_HEADS, HEAD_DIM)
    return (yp.reshape(bp, tp, d), ys.reshape(bs, ts, d), st_p[None], st_s[None],
            pages_p(kp_t), pages_p(vp_t), ks.reshape(new_s), vs.reshape(new_s))
```

```python
import functools

import jax
import jax.numpy as jnp
from jax import lax
from jax.experimental import pallas as pl
from jax.experimental.pallas import tpu as pltpu

N_HEADS = 16
HEAD_DIM = 64
CONV_WIDTH = 31
CONV_STATE = CONV_WIDTH - 1
PAGE_SIZE = 128
RMS_EPS = 1e-6
LN_EPS = 1e-5
LOG2_E = 1.4426950408889634

F32 = jnp.float32
BF16 = jnp.bfloat16

LANES = 128
SUBLANES = 8
MXU_DIM = 256
HALO = 32
VMEM_LIMIT = 56 * 1024 * 1024


def _resident(shape):
    nd = len(shape)
    return pl.BlockSpec(shape, lambda *_: (0,) * nd, pipeline_mode=pl.Buffered(1))


def _rmsnorm(x, g):
    return x * lax.rsqrt(jnp.mean(x * x, axis=-1, keepdims=True) + RMS_EPS) * g


def _sigmoid(x):
    return 1.0 / (1.0 + jnp.exp(-x))


def _dot(a, b):
    return jnp.dot(a, b, preferred_element_type=F32)


def _dot_nt(a, b):
    return lax.dot_general(a, b, (((1,), (1,)), ((), ())), preferred_element_type=F32)


def _softplus(z):
    return jnp.maximum(z, 0.0) + jnp.log(1.0 + jnp.exp2(jnp.abs(z) * -LOG2_E))


def _glu(h, w1_ref, b1_ref, d):
    a = _dot(h, w1_ref[:, :d]) + b1_ref[:, :d]
    gate = _dot(h, w1_ref[:, d:]) + b1_ref[:, d:]
    return a * _sigmoid(gate)


def _ln_silu_pw2(c, lng_ref, lnb_ref, w2_ref, b2_ref):
    mu = jnp.mean(c, axis=-1, keepdims=True)
    cc = c - mu
    var = jnp.mean(cc * cc, axis=-1, keepdims=True)
    y = cc * lax.rsqrt(var + LN_EPS) * lng_ref[...] + lnb_ref[...]
    y = y * _sigmoid(y)
    return _dot(y.astype(BF16), w2_ref[...]) + b2_ref[...]


def _conv_prompt_kernel(x_ref, g_ref, w1_ref, b1_ref, wdw_ref, bdw_ref, lng_ref, lnb_ref, w2_ref, b2_ref,
                        y_ref, st_ref, ubuf_ref, cbuf_ref, *, rows, cols):
    tt, d = x_ref.shape

    @pl.when(pl.program_id(1) == 0)
    def _():
        ubuf_ref[0:HALO, :] = jnp.zeros((HALO, d), F32)

    x = x_ref[...]
    h = _rmsnorm(x, g_ref[...]).astype(BF16)
    ubuf_ref[HALO:HALO + tt, :] = _glu(h, w1_ref, b1_ref, d)

    off = HALO - CONV_STATE
    for c0 in range(0, d, cols):
        for r0 in range(0, tt, rows):
            acc = jnp.broadcast_to(bdw_ref[:, c0:c0 + cols], (rows, cols))
            for phase in range(SUBLANES):
                taps = [m for m in range(phase, HALO + 1, SUBLANES) if 0 <= m - off < CONV_WIDTH]
                if not taps:
                    continue
                slab = ubuf_ref[r0 + phase:r0 + phase + rows + taps[-1] - phase, c0:c0 + cols]
                for m in taps:
                    w = wdw_ref[m - off, :, c0:c0 + cols]
                    seg = slab[m - phase:m - phase + rows].reshape(rows // SUBLANES, SUBLANES, cols)
                    acc = acc + (seg * w[None]).reshape(rows, cols)
            cbuf_ref[r0:r0 + rows, c0:c0 + cols] = acc

    y_ref[...] = x + _ln_silu_pw2(cbuf_ref[...], lng_ref, lnb_ref, w2_ref, b2_ref)
    st_ref[...] = ubuf_ref[tt + off:tt + HALO, :]
    ubuf_ref[0:HALO, :] = ubuf_ref[tt:tt + HALO, :]


def _conv_sample_kernel(x_ref, cache_ref, g_ref, w1_ref, b1_ref, wdw_ref, bdw_ref, lng_ref, lnb_ref, w2_ref, b2_ref,
                        y_ref, st_ref, ext_ref, cbuf_ref, *, cols):
    n, d = x_ref.shape
    nb, ts = ext_ref.shape[0], ext_ref.shape[1] - HALO
    off = HALO - CONV_STATE
    x = x_ref[...]
    h = _rmsnorm(x, g_ref[...]).astype(BF16)
    ext_ref[:, off:HALO, :] = cache_ref[...]
    ext_ref[:, HALO:, :] = _glu(h, w1_ref, b1_ref, d).reshape(nb, ts, d)
    for c0 in range(0, d, cols):
        acc = jnp.broadcast_to(bdw_ref[:, c0:c0 + cols][None], (nb, ts, cols))
        for k in range(CONV_WIDTH):
            acc = acc + ext_ref[:, off + k:off + k + ts, c0:c0 + cols] * wdw_ref[k, :, c0:c0 + cols][None]
        cbuf_ref[:, c0:c0 + cols] = acc.reshape(n, cols)
    y_ref[...] = x + _ln_silu_pw2(cbuf_ref[...], lng_ref, lnb_ref, w2_ref, b2_ref)
    st_ref[...] = ext_ref[:, ts + off:, :]


def _conv_weights(g, w1, b1, wdw, bdw, lng, lnb, w2, b2):
    d = w2.shape[0]
    row = lambda v: v.reshape(1, -1).astype(F32)
    wdw8 = jnp.broadcast_to(wdw[:, None, :], (CONV_WIDTH, SUBLANES, d))
    return (row(g), w1.astype(BF16), row(b1), wdw8, row(bdw), row(lng), row(lnb), w2.astype(BF16), row(b2))


def _conv_weight_specs(d):
    return [_resident((1, d)), _resident((d, 2 * d)), _resident((1, 2 * d)), _resident((CONV_WIDTH, SUBLANES, d)),
            _resident((1, d)), _resident((1, d)), _resident((1, d)), _resident((d, d)), _resident((1, d))]


def _conv_prompt(x, weights, *, tile_t):
    b, t, d = x.shape
    tt = min(tile_t, t)
    assert t % tt == 0 and tt % 64 == 0 and d % 256 == 0
    kern = functools.partial(_conv_prompt_kernel, rows=64, cols=256)
    return pl.pallas_call(
        kern,
        grid=(b, t // tt),
        in_specs=[pl.BlockSpec((None, tt, d), lambda i, j: (i, j, 0))] + _conv_weight_specs(d),
        out_specs=[pl.BlockSpec((None, tt, d), lambda i, j: (i, j, 0)),
                   pl.BlockSpec((None, CONV_STATE, d), lambda i, j: (i, 0, 0))],
        out_shape=[jax.ShapeDtypeStruct((b, t, d), F32), jax.ShapeDtypeStruct((b, CONV_STATE, d), F32)],
        scratch_shapes=[pltpu.VMEM((HALO + tt, d), F32), pltpu.VMEM((tt, d), F32)],
        compiler_params=pltpu.CompilerParams(dimension_semantics=("arbitrary", "arbitrary"),
                                             vmem_limit_bytes=VMEM_LIMIT),
        name="conv_prompt",
    )(x, *weights)


def _conv_sample(x, cache, weights):
    nb, ts, d = x.shape
    assert ts == SUBLANES and cache.shape == (nb, CONV_STATE, d)
    n = nb * ts
    kern = functools.partial(_conv_sample_kernel, cols=128)
    y, st = pl.pallas_call(
        kern,
        grid=(1,),
        in_specs=[_resident((n, d)), _resident((nb, CONV_STATE, d))] + _conv_weight_specs(d),
        out_specs=[pl.BlockSpec((n, d), lambda i: (0, 0)), pl.BlockSpec((nb, CONV_STATE, d), lambda i: (0, 0, 0))],
        out_shape=[jax.ShapeDtypeStruct((n, d), F32), jax.ShapeDtypeStruct((nb, CONV_STATE, d), F32)],
        scratch_shapes=[pltpu.VMEM((nb, HALO + ts, d), F32), pltpu.VMEM((n, d), F32)],
        compiler_params=pltpu.CompilerParams(dimension_semantics=("arbitrary",), vmem_limit_bytes=VMEM_LIMIT),
        name="conv_sample",
    )(x.reshape(n, d), cache, *weights)
    return y.reshape(nb, ts, d), st


def _ffn_kernel(*refs, has_proj, has_final, ff_chunk):
    refs = list(refs)
    x_ref = refs.pop(0)
    if has_proj:
        o_ref, wo_ref = refs.pop(0), refs.pop(0)
    g_ref, wg_ref, wu_ref, wd_ref = refs.pop(0), refs.pop(0), refs.pop(0), refs.pop(0)
    if has_final:
        gf_ref = refs.pop(0)
    y_ref, h_ref = refs
    x = x_ref[...]
    if has_proj:
        x = x + _dot(o_ref[...].astype(BF16), wo_ref[...])
    h_ref[...] = _rmsnorm(x, g_ref[...]).astype(BF16)
    acc = x
    d_ff = wg_ref.shape[1]
    for c0 in range(0, d_ff, ff_chunk):
        h = h_ref[...]
        gate = _dot(h, wg_ref[:, c0:c0 + ff_chunk])
        up = _dot(h, wu_ref[:, c0:c0 + ff_chunk])
        act = (gate * _sigmoid(gate) * up).astype(BF16)
        acc = acc + _dot(act, wd_ref[c0:c0 + ff_chunk, :])
    if has_final:
        acc = _rmsnorm(acc, gf_ref[...])
    y_ref[...] = acc


def _ffn(x, g, wg, wu, wd, *, tile_m, proj=None, final_g=None):
    n, d = x.shape
    d_ff = wg.shape[1]
    tm = min(tile_m, n)
    assert n % tm == 0 and d_ff % MXU_DIM == 0
    tok = lambda last, dt=None: pl.BlockSpec((tm, last), lambda i: (i, 0))
    args, specs = [x], [tok(d)]
    if proj is not None:
        o, wo = proj
        args += [o, wo.astype(BF16)]
        specs += [tok(o.shape[1]), _resident(wo.shape)]
    args += [g.reshape(1, d), wg.astype(BF16), wu.astype(BF16), wd.astype(BF16)]
    specs += [_resident((1, d)), _resident((d, d_ff)), _resident((d, d_ff)), _resident((d_ff, d))]
    if final_g is not None:
        args.append(final_g.reshape(1, d))
        specs.append(_resident((1, d)))
    kern = functools.partial(_ffn_kernel, has_proj=proj is not None, has_final=final_g is not None, ff_chunk=MXU_DIM)
    return pl.pallas_call(
        kern,
        grid=(n // tm,),
        in_specs=specs,
        out_specs=tok(d),
        out_shape=jax.ShapeDtypeStruct((n, d), F32),
        scratch_shapes=[pltpu.VMEM((tm, d), BF16)],
        compiler_params=pltpu.CompilerParams(dimension_semantics=("parallel",), vmem_limit_bytes=VMEM_LIMIT),
        name="ffn",
    )(*args)


def _qkv_kernel(x_ref, g_ref, w_ref, *rest, paged):
    tm, d = x_ref.shape
    h = _rmsnorm(x_ref[...], g_ref[...]).astype(BF16)
    q = _dot(h, w_ref[:, :d]) * (HEAD_DIM ** -0.5)
    k = _dot(h, w_ref[:, d:2 * d])
    v = _dot(h, w_ref[:, 2 * d:])
    if paged:
        wt_ref, q_ref, kb_ref, vb_ref, kt_ref, vt_ref = rest
        kb_ref[...] = k.astype(BF16)
        vb_ref[...] = v.astype(BF16)
        for t_ref, r0 in ((kt_ref, 0), (vt_ref, d)):
            xt = _dot_nt(wt_ref[r0:r0 + d, :], h)
            for p in range(tm // PAGE_SIZE):
                t_ref[p] = xt[:, p * PAGE_SIZE:(p + 1) * PAGE_SIZE]
    else:
        q_ref, k_ref, v_ref = rest
        k_ref[...] = k
        v_ref[...] = v
    q_ref[...] = q.astype(q_ref.dtype)


def _qkv(x, g, w_qkv, *, tile_m, paged):
    n, d = x.shape
    assert w_qkv.shape == (d, 3 * d) and d == N_HEADS * HEAD_DIM
    tm = min(tile_m, n)
    assert n % tm == 0
    tok = pl.BlockSpec((tm, d), lambda i: (i, 0))
    f32o, bf16o = jax.ShapeDtypeStruct((n, d), F32), jax.ShapeDtypeStruct((n, d), BF16)
    args = [x, g.reshape(1, d), w_qkv.astype(BF16)]
    in_specs = [tok, _resident((1, d)), _resident((d, 3 * d))]
    if paged:
        assert tm % PAGE_SIZE == 0
        pages = pl.BlockSpec((tm // PAGE_SIZE, d, PAGE_SIZE), lambda i: (i, 0, 0))
        paged_o = jax.ShapeDtypeStruct((n // PAGE_SIZE, d, PAGE_SIZE), F32)
        args.append(jnp.transpose(w_qkv[:, d:]).astype(BF16))
        in_specs.append(_resident((2 * d, d)))
        out_shape, out_specs = [bf16o, bf16o, bf16o, paged_o, paged_o], [tok, tok, tok, pages, pages]
    else:
        out_shape, out_specs = [f32o, f32o, f32o], [tok, tok, tok]
    return pl.pallas_call(
        functools.partial(_qkv_kernel, paged=paged),
        grid=(n // tm,),
        in_specs=in_specs,
        out_specs=out_specs,
        out_shape=out_shape,
        compiler_params=pltpu.CompilerParams(dimension_semantics=("parallel",), vmem_limit_bytes=VMEM_LIMIT),
        name="qkv",
    )(*args)


def _sb_prompt_kernel(bias_ref, q_ref, k_ref, v_ref, o_ref,
                      qq_ref, tri_ref, u_ref, h_ref, rs_ref, r_ref, acc_ref, *, tq, tk):
    hp, qi = pl.program_id(1), pl.program_id(2)
    lane = lax.broadcasted_iota(jnp.int32, (1, LANES), 1)
    zero = jnp.zeros((tk, LANES), BF16)
    for c in range(4):
        qc = q_ref[(c // 2) * tk:(c // 2 + 1) * tk, :]
        qq_ref[c * tk:(c + 1) * tk] = jnp.where((lane >= HEAD_DIM) if c % 2 else (lane < HEAD_DIM), qc, zero)
    tri_ref[...] = (lax.broadcasted_iota(jnp.int32, (tk, tk), 0)
                    > lax.broadcasted_iota(jnp.int32, (tk, tk), 1)).astype(BF16)
    r_ref[...] = jnp.zeros_like(r_ref)
    acc_ref[...] = jnp.zeros_like(acc_ref)
    bias = (bias_ref[2 * hp], bias_ref[2 * hp + 1])

    def stage_a(kb, slot, masked, r0=0):
        start = pl.multiple_of(kb * tk, tk)
        z = _dot_nt(qq_ref[r0:], k_ref[pl.ds(start, tk), :])
        for c in range(r0 // tk, 4):
            rows = slice(c * tk, (c + 1) * tk)
            zc = z[c * tk - r0:(c + 1) * tk - r0] + bias[c % 2]
            sp = _softplus(zc)
            u = zc - sp
            if masked:
                q_pos = qi * tq + (c // 2) * tk + lax.broadcasted_iota(jnp.int32, (tk, tk), 0)
                k_pos = kb * tk + lax.broadcasted_iota(jnp.int32, (tk, tk), 1)
                causal = k_pos < q_pos
                sp = jnp.where(causal, sp, 0.0)
                u = jnp.where(causal, u, -jnp.inf)
            u_ref[slot, rows] = u
            h_ref[slot, rows] = sp.astype(BF16)
            rs_ref[slot, rows] = jnp.broadcast_to(jnp.sum(sp, axis=-1, keepdims=True), (tk, LANES))

    def stage_b(kb, slot, r0=0):
        start = pl.multiple_of(kb * tk, tk)
        after = _dot(h_ref[slot, r0:], tri_ref[...])
        r = r_ref[r0:]
        e = u_ref[slot, r0:] - after - jnp.concatenate([r] * (tk // LANES), axis=1)
        acc_ref[r0:] += _dot(jnp.exp(e).astype(BF16), v_ref[pl.ds(start, tk), :])
        r_ref[r0:] = r + rs_ref[slot, r0:]

    top = 2 * qi + 1
    stage_a(top, 0, True, r0=tq)
    stage_a(top - 1, 1, True)
    stage_b(top, 0, r0=tq)

    def pair(kb):
        stage_a(kb, 0, False)
        stage_b(kb + 1, 1)
        stage_a(kb - 1, 1, False)
        stage_b(kb, 0)

    odd = qi % 2

    @pl.when(odd == 1)
    def _():
        pair(2 * qi - 1)

    def quad(t, carry):
        kb = 2 * (qi - odd) - 1 - 4 * t
        pair(kb)
        pair(kb - 2)
        return carry

    lax.fori_loop(0, qi // 2, quad, 0)
    stage_b(0, 1)
    for half in range(2):
        lo, hi = acc_ref[2 * half * tk:(2 * half + 1) * tk], acc_ref[(2 * half + 1) * tk:(2 * half + 2) * tk]
        o_ref[half * tk:(half + 1) * tk, :] = jnp.where(lane < HEAD_DIM, lo, hi).astype(o_ref.dtype)


def _sb_prompt(q, k, v, bias, *, tq, tk):
    b, t, d = q.shape
    assert t % tq == 0 and tq == 2 * tk and tk % LANES == 0
    assert d == N_HEADS * HEAD_DIM and 2 * HEAD_DIM == LANES
    kv_spec = pl.BlockSpec((None, t, LANES), lambda i, h, j: (i, 0, h))
    q_spec = pl.BlockSpec((None, tq, LANES), lambda i, h, j: (i, j, h))
    m = 2 * tq
    return pl.pallas_call(
        functools.partial(_sb_prompt_kernel, tq=tq, tk=tk),
        grid=(b, N_HEADS // 2, t // tq),
        in_specs=[pl.BlockSpec(memory_space=pltpu.SMEM), q_spec, kv_spec, kv_spec],
        out_specs=q_spec,
        out_shape=jax.ShapeDtypeStruct((b, t, d), BF16),
        scratch_shapes=[pltpu.VMEM((m, LANES), BF16), pltpu.VMEM((tk, tk), BF16),
                        pltpu.VMEM((2, m, tk), F32), pltpu.VMEM((2, m, tk), BF16),
                        pltpu.VMEM((2, m, LANES), F32), pltpu.VMEM((m, LANES), F32), pltpu.VMEM((m, LANES), F32)],
        compiler_params=pltpu.CompilerParams(dimension_semantics=("parallel", "parallel", "arbitrary"),
                                             vmem_limit_bytes=VMEM_LIMIT),
        name="sb_prompt",
    )(bias.astype(F32), q, k, v)


def _sb_sample_kernel(pt_ref, bias_ref, q_ref, kn_ref, vn_ref, kpool_ref, vpool_ref, o_ref,
                      kbuf_ref, vbuf_ref, pad_ref, qbd_ref, r_ref, acc_ref, sem_ref, *, n_chunks, pages):
    n = pl.program_id(0)
    n_steps = pl.num_programs(0)
    ts, d = q_ref.shape
    hq = N_HEADS * ts
    n_pages = n_chunks * pages
    blk = MXU_DIM

    def copies(step, slot):
        seq = step // n_chunks
        first = (n_chunks - 1 - step % n_chunks) * pages
        out = []
        for p in range(pages):
            page = pt_ref[seq * n_pages + first + p]
            out.append(pltpu.make_async_copy(kpool_ref.at[page], kbuf_ref.at[slot, p], sem_ref.at[0, slot]))
            out.append(pltpu.make_async_copy(vpool_ref.at[page], vbuf_ref.at[slot, p], sem_ref.at[1, slot]))
        return out

    slot = n % 2

    @pl.when(n == 0)
    def _():
        for cp in copies(n, slot):
            cp.start()

    @pl.when(n + 1 < n_steps)
    def _():
        for cp in copies(n + 1, 1 - slot):
            cp.start()

    row = lax.broadcasted_iota(jnp.int32, (blk, blk), 0)
    col = lax.broadcasted_iota(jnp.int32, (blk, blk), 1)
    tri = (row > col).astype(BF16)

    def fold(z, valid, times_v):
        w = z.shape[1]
        step = min(blk, w)
        sp = _softplus(z)
        u = z - sp
        if valid is not None:
            sp = jnp.where(valid, sp, 0.0)
            u = jnp.where(valid, u, -jnp.inf)
        r = r_ref[...]
        es = [None] * (w // step)
        for j in reversed(range(w // step)):
            spj = sp[:, j * step:(j + 1) * step]
            after = _dot(spj.astype(BF16), tri[:step, :step])
            es[j] = u[:, j * step:(j + 1) * step] - after - r
            r = r + jnp.sum(spj, axis=-1, keepdims=True)
        acc_ref[...] += times_v(jnp.exp(jnp.concatenate(es, axis=1)).astype(BF16))
        r_ref[...] = r

    @pl.when(n % n_chunks == 0)
    def _():
        q = jnp.concatenate([q_ref[...]] * N_HEADS, axis=0)
        qr = lax.broadcasted_iota(jnp.int32, (hq, d), 0)
        qc = lax.broadcasted_iota(jnp.int32, (hq, d), 1)
        qbd_ref[...] = jnp.where(qr // ts == qc // HEAD_DIM, q, 0.0).astype(BF16)
        r_ref[...] = jnp.zeros_like(r_ref)
        acc_ref[...] = jnp.zeros_like(acc_ref)
        pad_ref[...] = jnp.zeros_like(pad_ref)
        pad_ref[0, 0:ts, :] = kn_ref[...]
        pad_ref[1, 0:ts, :] = vn_ref[...]
        qry = lax.broadcasted_iota(jnp.int32, (hq, PAGE_SIZE), 0) % ts
        key = lax.broadcasted_iota(jnp.int32, (hq, PAGE_SIZE), 1)
        z = _dot_nt(qbd_ref[...], pad_ref[0].astype(BF16)) + bias_ref[...]
        fold(z, key < qry, lambda a: _dot(a, pad_ref[1].astype(BF16)))

    for cp in copies(n, slot):
        cp.wait()
    all_pages = lambda buf: jnp.concatenate([buf[slot, p] for p in range(pages)], axis=1).astype(BF16)
    z = _dot(qbd_ref[...], all_pages(kbuf_ref)) + bias_ref[...]
    fold(z, None, lambda a: _dot_nt(a, all_pages(vbuf_ref)))

    @pl.when(n % n_chunks == n_chunks - 1)
    def _():
        orow = lax.broadcasted_iota(jnp.int32, (hq, d), 0)
        ocol = lax.broadcasted_iota(jnp.int32, (hq, d), 1)
        o = jnp.where(orow // ts == ocol // HEAD_DIM, acc_ref[...], 0.0)
        o_ref[...] = jnp.sum(o.reshape(N_HEADS, ts, d), axis=0)


def _sb_sample(q, k_new, v_new, k_pool, v_pool, page_table, bias, *, pages):
    nb, ts, d = q.shape
    n_pages = page_table.shape[1]
    pages = min(pages, n_pages)
    group = MXU_DIM // PAGE_SIZE
    assert ts == SUBLANES and n_pages % pages == 0 and pages % group == 0 and N_HEADS * ts == LANES
    assert k_pool.shape[1:] == (PAGE_SIZE, N_HEADS, HEAD_DIM)
    n_chunks = n_pages // pages
    n_pool = k_pool.shape[0]
    hq = N_HEADS * ts
    pool_t = lambda a: jnp.transpose(a, (0, 2, 3, 1)).reshape(n_pool, d, PAGE_SIZE)
    tok = pl.BlockSpec((None, ts, d), lambda i, pt: (i // n_chunks, 0, 0))
    grid_spec = pltpu.PrefetchScalarGridSpec(
        num_scalar_prefetch=1,
        grid=(nb * n_chunks,),
        in_specs=[pl.BlockSpec((hq, 1), lambda i, pt: (0, 0)), tok, tok, tok,
                  pl.BlockSpec(memory_space=pl.ANY), pl.BlockSpec(memory_space=pl.ANY)],
        out_specs=tok,
        scratch_shapes=[pltpu.VMEM((2, pages, d, PAGE_SIZE), F32), pltpu.VMEM((2, pages, d, PAGE_SIZE), F32),
                        pltpu.VMEM((2, PAGE_SIZE, d), F32), pltpu.VMEM((hq, d), BF16),
                        pltpu.VMEM((hq, 1), F32), pltpu.VMEM((hq, d), F32),
                        pltpu.SemaphoreType.DMA((2, 2))],
    )
    return pl.pallas_call(
        functools.partial(_sb_sample_kernel, n_chunks=n_chunks, pages=pages),
        grid_spec=grid_spec,
        out_shape=jax.ShapeDtypeStruct((nb, ts, d), F32),
        compiler_params=pltpu.CompilerParams(dimension_semantics=("arbitrary",), vmem_limit_bytes=VMEM_LIMIT),
        name="sb_sample",
    )(page_table.reshape(-1), jnp.repeat(bias.astype(F32), ts).reshape(hq, 1), q, k_new, v_new,
      pool_t(k_pool), pool_t(v_pool))


def kernel(x_prompt, x_sample, cache_conv, cache_k, cache_v, page_table, mix_norm_g, ffn_norm_g, final_norm_g,
           cv_w_pw1, cv_b_pw1, cv_w_dw, cv_b_dw, cv_ln_g, cv_ln_b, cv_w_pw2, cv_b_pw2,
           sb_w_qkv, sb_w_o, sb_logit_bias, ffn_w_gate, ffn_w_up, ffn_w_down):
    bp, tp, d = x_prompt.shape
    bs, ts, _ = x_sample.shape
    assert mix_norm_g.shape[0] == 2 and tp % PAGE_SIZE == 0

    cw = _conv_weights(mix_norm_g[0], cv_w_pw1[0], cv_b_pw1[0], cv_w_dw[0], cv_b_dw[0], cv_ln_g[0], cv_ln_b[0],
                       cv_w_pw2[0], cv_b_pw2[0])
    xp, st_p = _conv_prompt(x_prompt, cw, tile_t=256)
    xs, st_s = _conv_sample(x_sample, cache_conv[0], cw)
    ffn0 = (ffn_norm_g[0], ffn_w_gate[0], ffn_w_up[0], ffn_w_down[0])
    xp = _ffn(xp.reshape(bp * tp, d), *ffn0, tile_m=512)
    xs = _ffn(xs.reshape(bs * ts, d), *ffn0, tile_m=512)

    qp, kpb, vpb, kp_t, vp_t = _qkv(xp, mix_norm_g[1], sb_w_qkv[0], tile_m=512, paged=True)
    qs, ks, vs = _qkv(xs, mix_norm_g[1], sb_w_qkv[0], tile_m=512, paged=False)
    seq = lambda a: a.reshape(bp, tp, d)
    op = _sb_prompt(seq(qp), seq(kpb), seq(vpb), sb_logit_bias[0], tq=512, tk=256)
    dec = lambda a: a.reshape(bs, ts, d)
    os_ = _sb_sample(dec(qs), dec(ks), dec(vs), cache_k[0], cache_v[0], page_table, sb_logit_bias[0], pages=16)
    ffn1 = (ffn_norm_g[1], ffn_w_gate[1], ffn_w_up[1], ffn_w_down[1])
    yp = _ffn(xp, *ffn1, tile_m=512, proj=(op.reshape(bp * tp, d), sb_w_o[0]), final_g=final_norm_g)
    ys = _ffn(xs, *ffn1, tile_m=512, proj=(os_.reshape(bs * ts, d), sb_w_o[0]), final_g=final_norm_g)

    pages_p = lambda a: jnp.transpose(a.reshape(1, bp, tp // PAGE_SIZE, N_HEADS, HEAD_DIM, PAGE_SIZE), (0, 1, 2, 5, 3, 4))
    new_s = (1, bs, ts, N_HEADS, HEAD_DIM)
    return (yp.reshape(bp, tp, d), ys.reshape(bs, ts, d), st_p[None], st_s[None],
            pages_p(kp_t), pages_p(vp_t), ks.reshape(new_s), vs.reshape(new_s))
```

```python
import functools

import jax
import jax.numpy as jnp
from jax import lax
from jax.experimental import pallas as pl
from jax.experimental.pallas import tpu as pltpu

N_HEADS = 16
HEAD_DIM = 64
CONV_WIDTH = 31
CONV_STATE = CONV_WIDTH - 1
PAGE_SIZE = 128
RMS_EPS = 1e-6
LN_EPS = 1e-5
LOG2_E = 1.4426950408889634

F32 = jnp.float32
BF16 = jnp.bfloat16

LANES = 128
SUBLANES = 8
MXU_DIM = 256
HALO = 32
VMEM_LIMIT = 56 * 1024 * 1024


def _resident(shape):
    nd = len(shape)
    return pl.BlockSpec(shape, lambda *_: (0,) * nd, pipeline_mode=pl.Buffered(1))


def _rmsnorm(x, g):
    return x * lax.rsqrt(jnp.mean(x * x, axis=-1, keepdims=True) + RMS_EPS) * g


def _sigmoid(x):
    return 1.0 / (1.0 + jnp.exp(-x))


def _dot(a, b):
    return jnp.dot(a, b, preferred_element_type=F32)


def _dot_nt(a, b):
    return lax.dot_general(a, b, (((1,), (1,)), ((), ())), preferred_element_type=F32)


def _softplus(z):
    return jnp.maximum(z, 0.0) + jnp.log(1.0 + jnp.exp2(jnp.abs(z) * -LOG2_E))


def _glu(h, w1_ref, b1_ref, d):
    a = _dot(h, w1_ref[:, :d]) + b1_ref[:, :d]
    gate = _dot(h, w1_ref[:, d:]) + b1_ref[:, d:]
    return a * _sigmoid(gate)


def _ln_silu_pw2(c, lng_ref, lnb_ref, w2_ref, b2_ref):
    mu = jnp.mean(c, axis=-1, keepdims=True)
    cc = c - mu
    var = jnp.mean(cc * cc, axis=-1, keepdims=True)
    y = cc * lax.rsqrt(var + LN_EPS) * lng_ref[...] + lnb_ref[...]
    y = y * _sigmoid(y)
    return _dot(y.astype(BF16), w2_ref[...]) + b2_ref[...]


def _conv_prompt_kernel(x_ref, g_ref, w1_ref, b1_ref, wdw_ref, bdw_ref, lng_ref, lnb_ref, w2_ref, b2_ref,
                        y_ref, st_ref, ubuf_ref, cbuf_ref, *, rows, cols):
    tt, d = x_ref.shape

    @pl.when(pl.program_id(1) == 0)
    def _():
        ubuf_ref[0:HALO, :] = jnp.zeros((HALO, d), F32)

    x = x_ref[...]
    h = _rmsnorm(x, g_ref[...]).astype(BF16)
    ubuf_ref[HALO:HALO + tt, :] = _glu(h, w1_ref, b1_ref, d)

    off = HALO - CONV_STATE
    for c0 in range(0, d, cols):
        for r0 in range(0, tt, rows):
            acc = jnp.broadcast_to(bdw_ref[:, c0:c0 + cols], (rows, cols))
            for phase in range(SUBLANES):
                taps = [m for m in range(phase, HALO + 1, SUBLANES) if 0 <= m - off < CONV_WIDTH]
                if not taps:
                    continue
                slab = ubuf_ref[r0 + phase:r0 + phase + rows + taps[-1] - phase, c0:c0 + cols]
                for m in taps:
                    w = wdw_ref[m - off, :, c0:c0 + cols]
                    seg = slab[m - phase:m - phase + rows].reshape(rows // SUBLANES, SUBLANES, cols)
                    acc = acc + (seg * w[None]).reshape(rows, cols)
            cbuf_ref[r0:r0 + rows, c0:c0 + cols] = acc

    y_ref[...] = x + _ln_silu_pw2(cbuf_ref[...], lng_ref, lnb_ref, w2_ref, b2_ref)
    st_ref[...] = ubuf_ref[tt + off:tt + HALO, :]
    ubuf_ref[0:HALO, :] = ubuf_ref[tt:tt + HALO, :]


def _conv_sample_kernel(x_ref, cache_ref, g_ref, w1_ref, b1_ref, wdw_ref, bdw_ref, lng_ref, lnb_ref, w2_ref, b2_ref,
                        y_ref, st_ref, ext_ref, cbuf_ref, *, cols):
    n, d = x_ref.shape
    nb, ts = ext_ref.shape[0], ext_ref.shape[1] - HALO
    off = HALO - CONV_STATE
    x = x_ref[...]
    h = _rmsnorm(x, g_ref[...]).astype(BF16)
    ext_ref[:, off:HALO, :] = cache_ref[...]
    ext_ref[:, HALO:, :] = _glu(h, w1_ref, b1_ref, d).reshape(nb, ts, d)
    for c0 in range(0, d, cols):
        acc = jnp.broadcast_to(bdw_ref[:, c0:c0 + cols][None], (nb, ts, cols))
        for k in range(CONV_WIDTH):
            acc = acc + ext_ref[:, off + k:off + k + ts, c0:c0 + cols] * wdw_ref[k, :, c0:c0 + cols][None]
        cbuf_ref[:, c0:c0 + cols] = acc.reshape(n, cols)
    y_ref[...] = x + _ln_silu_pw2(cbuf_ref[...], lng_ref, lnb_ref, w2_ref, b2_ref)
    st_ref[...] = ext_ref[:, ts + off:, :]


def _conv_weights(g, w1, b1, wdw, bdw, lng, lnb, w2, b2):
    d = w2.shape[0]
    row = lambda v: v.reshape(1, -1).astype(F32)
    wdw8 = jnp.broadcast_to(wdw[:, None, :], (CONV_WIDTH, SUBLANES, d))
    return (row(g), w1.astype(BF16), row(b1), wdw8, row(bdw), row(lng), row(lnb), w2.astype(BF16), row(b2))


def _conv_weight_specs(d):
    return [_resident((1, d)), _resident((d, 2 * d)), _resident((1, 2 * d)), _resident((CONV_WIDTH, SUBLANES, d)),
            _resident((1, d)), _resident((1, d)), _resident((1, d)), _resident((d, d)), _resident((1, d))]


def _conv_prompt(x, weights, *, tile_t):
    b, t, d = x.shape
    tt = min(tile_t, t)
    assert t % tt == 0 and tt % 64 == 0 and d % 256 == 0
    kern = functools.partial(_conv_prompt_kernel, rows=64, cols=256)
    return pl.pallas_call(
        kern,
        grid=(b, t // tt),
        in_specs=[pl.BlockSpec((None, tt, d), lambda i, j: (i, j, 0))] + _conv_weight_specs(d),
        out_specs=[pl.BlockSpec((None, tt, d), lambda i, j: (i, j, 0)),
                   pl.BlockSpec((None, CONV_STATE, d), lambda i, j: (i, 0, 0))],
        out_shape=[jax.ShapeDtypeStruct((b, t, d), F32), jax.ShapeDtypeStruct((b, CONV_STATE, d), F32)],
        scratch_shapes=[pltpu.VMEM((HALO + tt, d), F32), pltpu.VMEM((tt, d), F32)],
        compiler_params=pltpu.CompilerParams(dimension_semantics=("arbitrary", "arbitrary"),
                                             vmem_limit_bytes=VMEM_LIMIT),
        name="conv_prompt",
    )(x, *weights)


def _conv_sample(x, cache, weights):
    nb, ts, d = x.shape
    assert ts == SUBLANES and cache.shape == (nb, CONV_STATE, d)
    n = nb * ts
    kern = functools.partial(_conv_sample_kernel, cols=128)
    y, st = pl.pallas_call(
        kern,
        grid=(1,),
        in_specs=[_resident((n, d)), _resident((nb, CONV_STATE, d))] + _conv_weight_specs(d),
        out_specs=[pl.BlockSpec((n, d), lambda i: (0, 0)), pl.BlockSpec((nb, CONV_STATE, d), lambda i: (0, 0, 0))],
        out_shape=[jax.ShapeDtypeStruct((n, d), F32), jax.ShapeDtypeStruct((nb, CONV_STATE, d), F32)],
        scratch_shapes=[pltpu.VMEM((nb, HALO + ts, d), F32), pltpu.VMEM((n, d), F32)],
        compiler_params=pltpu.CompilerParams(dimension_semantics=("arbitrary",), vmem_limit_bytes=VMEM_LIMIT),
        name="conv_sample",
    )(x.reshape(n, d), cache, *weights)
    return y.reshape(nb, ts, d), st


def _ffn_kernel(*refs, has_proj, has_final, ff_chunk):
    refs = list(refs)
    x_ref = refs.pop(0)
    if has_proj:
        o_ref, wo_ref = refs.pop(0), refs.pop(0)
    g_ref, wg_ref, wu_ref, wd_ref = refs.pop(0), refs.pop(0), refs.pop(0), refs.pop(0)
    if has_final:
        gf_ref = refs.pop(0)
    y_ref, h_ref = refs
    x = x_ref[...]
    if has_proj:
        x = x + _dot(o_ref[...].astype(BF16), wo_ref[...])
    h_ref[...] = _rmsnorm(x, g_ref[...]).astype(BF16)
    acc = x
    d_ff = wg_ref.shape[1]
    for c0 in range(0, d_ff, ff_chunk):
        h = h_ref[...]
        gate = _dot(h, wg_ref[:, c0:c0 + ff_chunk])
        up = _dot(h, wu_ref[:, c0:c0 + ff_chunk])
        act = (gate * _sigmoid(gate) * up).astype(BF16)
        acc = acc + _dot(act, wd_ref[c0:c0 + ff_chunk, :])
    if has_final:
        acc = _rmsnorm(acc, gf_ref[...])
    y_ref[...] = acc


def _ffn(x, g, wg, wu, wd, *, tile_m, proj=None, final_g=None):
    n, d = x.shape
    d_ff = wg.shape[1]
    tm = min(tile_m, n)
    assert n % tm == 0 and d_ff % MXU_DIM == 0
    tok = lambda last, dt=None: pl.BlockSpec((tm, last), lambda i: (i, 0))
    args, specs = [x], [tok(d)]
    if proj is not None:
        o, wo = proj
        args += [o, wo.astype(BF16)]
        specs += [tok(o.shape[1]), _resident(wo.shape)]
    args += [g.reshape(1, d), wg.astype(BF16), wu.astype(BF16), wd.astype(BF16)]
    specs += [_resident((1, d)), _resident((d, d_ff)), _resident((d, d_ff)), _resident((d_ff, d))]
    if final_g is not None:
        args.append(final_g.reshape(1, d))
        specs.append(_resident((1, d)))
    kern = functools.partial(_ffn_kernel, has_proj=proj is not None, has_final=final_g is not None, ff_chunk=MXU_DIM)
    return pl.pallas_call(
        kern,
        grid=(n // tm,),
        in_specs=specs,
        out_specs=tok(d),
        out_shape=jax.ShapeDtypeStruct((n, d), F32),
        scratch_shapes=[pltpu.VMEM((tm, d), BF16)],
        compiler_params=pltpu.CompilerParams(dimension_semantics=("parallel",), vmem_limit_bytes=VMEM_LIMIT),
        name="ffn",
    )(*args)


def _qkv_kernel(x_ref, g_ref, w_ref, *rest, paged):
    tm, d = x_ref.shape
    h = _rmsnorm(x_ref[...], g_ref[...]).astype(BF16)
    q = _dot(h, w_ref[:, :d]) * (HEAD_DIM ** -0.5)
    k = _dot(h, w_ref[:, d:2 * d])
    v = _dot(h, w_ref[:, 2 * d:])
    if paged:
        wt_ref, q_ref, kb_ref, vb_ref, kt_ref, vt_ref = rest
        kb_ref[...] = k.astype(BF16)
        vb_ref[...] = v.astype(BF16)
        for t_ref, r0 in ((kt_ref, 0), (vt_ref, d)):
            xt = _dot_nt(wt_ref[r0:r0 + d, :], h)
            for p in range(tm // PAGE_SIZE):
                t_ref[p] = xt[:, p * PAGE_SIZE:(p + 1) * PAGE_SIZE]
    else:
        q_ref, k_ref, v_ref = rest
        k_ref[...] = k
        v_ref[...] = v
    q_ref[...] = q.astype(q_ref.dtype)


def _qkv(x, g, w_qkv, *, tile_m, paged):
    n, d = x.shape
    assert w_qkv.shape == (d, 3 * d) and d == N_HEADS * HEAD_DIM
    tm = min(tile_m, n)
    assert n % tm == 0
    tok = pl.BlockSpec((tm, d), lambda i: (i, 0))
    f32o, bf16o = jax.ShapeDtypeStruct((n, d), F32), jax.ShapeDtypeStruct((n, d), BF16)
    args = [x, g.reshape(1, d), w_qkv.astype(BF16)]
    in_specs = [tok, _resident((1, d)), _resident((d, 3 * d))]
    if paged:
        assert tm % PAGE_SIZE == 0
        pages = pl.BlockSpec((tm // PAGE_SIZE, d, PAGE_SIZE), lambda i: (i, 0, 0))
        paged_o = jax.ShapeDtypeStruct((n // PAGE_SIZE, d, PAGE_SIZE), F32)
        args.append(jnp.transpose(w_qkv[:, d:]).astype(BF16))
        in_specs.append(_resident((2 * d, d)))
        out_shape, out_specs = [bf16o, bf16o, bf16o, paged_o, paged_o], [tok, tok, tok, pages, pages]
    else:
        out_shape, out_specs = [f32o, f32o, f32o], [tok, tok, tok]
    return pl.pallas_call(
        functools.partial(_qkv_kernel, paged=paged),
        grid=(n // tm,),
        in_specs=in_specs,
        out_specs=out_specs,
        out_shape=out_shape,
        compiler_params=pltpu.CompilerParams(dimension_semantics=("parallel",), vmem_limit_bytes=VMEM_LIMIT),
        name="qkv",
    )(*args)


def _sb_prompt_step(hp, qi, bias_ref, q_ref, k_ref, v_ref, o_ref,
                    qq_ref, tri_ref, u_ref, h_ref, rs_ref, r_ref, acc_ref, *, tq, tk):
    lane = lax.broadcasted_iota(jnp.int32, (1, LANES), 1)
    zero = jnp.zeros((tk, LANES), BF16)
    for c in range(4):
        qc = q_ref[(c // 2) * tk:(c // 2 + 1) * tk, :]
        qq_ref[c * tk:(c + 1) * tk] = jnp.where((lane >= HEAD_DIM) if c % 2 else (lane < HEAD_DIM), qc, zero)
    tri_ref[...] = (lax.broadcasted_iota(jnp.int32, (tk, tk), 0)
                    > lax.broadcasted_iota(jnp.int32, (tk, tk), 1)).astype(BF16)
    r_ref[...] = jnp.zeros_like(r_ref)
    acc_ref[...] = jnp.zeros_like(acc_ref)
    bias = (bias_ref[2 * hp], bias_ref[2 * hp + 1])

    def stage_a(kb, slot, masked, r0=0):
        start = pl.multiple_of(kb * tk, tk)
        z = _dot_nt(qq_ref[r0:], k_ref[pl.ds(start, tk), :])
        for c in range(r0 // tk, 4):
            rows = slice(c * tk, (c + 1) * tk)
            zc = z[c * tk - r0:(c + 1) * tk - r0] + bias[c % 2]
            sp = _softplus(zc)
            u = zc - sp
            if masked:
                q_pos = qi * tq + (c // 2) * tk + lax.broadcasted_iota(jnp.int32, (tk, tk), 0)
                k_pos = kb * tk + lax.broadcasted_iota(jnp.int32, (tk, tk), 1)
                causal = k_pos < q_pos
                sp = jnp.where(causal, sp, 0.0)
                u = jnp.where(causal, u, -jnp.inf)
            u_ref[slot, rows] = u
            h_ref[slot, rows] = sp.astype(BF16)
            rs_ref[slot, rows] = jnp.broadcast_to(jnp.sum(sp, axis=-1, keepdims=True), (tk, LANES))

    def stage_b(kb, slot, r0=0):
        start = pl.multiple_of(kb * tk, tk)
        after = _dot(h_ref[slot, r0:], tri_ref[...])
        r = r_ref[r0:]
        e = u_ref[slot, r0:] - after - jnp.concatenate([r] * (tk // LANES), axis=1)
        acc_ref[r0:] += _dot(jnp.exp(e).astype(BF16), v_ref[pl.ds(start, tk), :])
        r_ref[r0:] = r + rs_ref[slot, r0:]

    top = 2 * qi + 1
    stage_a(top, 0, True, r0=tq)
    stage_a(top - 1, 1, True)
    stage_b(top, 0, r0=tq)

    def pair(kb):
        stage_a(kb, 0, False)
        stage_b(kb + 1, 1)
        stage_a(kb - 1, 1, False)
        stage_b(kb, 0)

    odd = qi % 2

    @pl.when(odd == 1)
    def _():
        pair(2 * qi - 1)

    def quad(t, carry):
        kb = 2 * (qi - odd) - 1 - 4 * t
        pair(kb)
        pair(kb - 2)
        return carry

    lax.fori_loop(0, qi // 2, quad, 0)
    stage_b(0, 1)
    for half in range(2):
        lo, hi = acc_ref[2 * half * tk:(2 * half + 1) * tk], acc_ref[(2 * half + 1) * tk:(2 * half + 2) * tk]
        o_ref[half * tk:(half + 1) * tk, :] = jnp.where(lane < HEAD_DIM, lo, hi).astype(o_ref.dtype)


def _page_copies(step, slot, pt_ref, kpool_ref, vpool_ref, kbuf_ref, vbuf_ref, sem_ref, *, n_chunks, pages):
    n_pages = n_chunks * pages
    seq = step // n_chunks
    first = (n_chunks - 1 - step % n_chunks) * pages
    out = []
    for p in range(pages):
        page = pt_ref[seq * n_pages + first + p]
        out.append(pltpu.make_async_copy(kpool_ref.at[page], kbuf_ref.at[slot, p], sem_ref.at[0, slot]))
        out.append(pltpu.make_async_copy(vpool_ref.at[page], vbuf_ref.at[slot, p], sem_ref.at[1, slot]))
    return out


def _sb_sample_prefetch(n, n_steps, *dma_refs, n_chunks, pages):
    copies = functools.partial(_page_copies, n_chunks=n_chunks, pages=pages)

    @pl.when(n == 0)
    def _():
        for cp in copies(n, n % 2, *dma_refs):
            cp.start()

    @pl.when(n + 1 < n_steps)
    def _():
        for cp in copies(n + 1, 1 - n % 2, *dma_refs):
            cp.start()


def _sb_sample_step(n, pt_ref, bias_ref, q_ref, kn_ref, vn_ref, kpool_ref, vpool_ref, o_ref,
                    kbuf_ref, vbuf_ref, pad_ref, qbd_ref, r_ref, acc_ref, sem_ref, *, n_chunks, pages):
    ts, d = q_ref.shape
    hq = N_HEADS * ts
    blk = MXU_DIM
    slot = n % 2
    row = lax.broadcasted_iota(jnp.int32, (blk, blk), 0)
    col = lax.broadcasted_iota(jnp.int32, (blk, blk), 1)
    tri = (row > col).astype(BF16)

    def fold(z, valid, times_v):
        w = z.shape[1]
        step = min(blk, w)
        sp = _softplus(z)
        u = z - sp
        if valid is not None:
            sp = jnp.where(valid, sp, 0.0)
            u = jnp.where(valid, u, -jnp.inf)
        r = r_ref[...]
        es = [None] * (w // step)
        for j in reversed(range(w // step)):
            spj = sp[:, j * step:(j + 1) * step]
            after = _dot(spj.astype(BF16), tri[:step, :step])
            es[j] = u[:, j * step:(j + 1) * step] - after - r
            r = r + jnp.sum(spj, axis=-1, keepdims=True)
        acc_ref[...] += times_v(jnp.exp(jnp.concatenate(es, axis=1)).astype(BF16))
        r_ref[...] = r

    @pl.when(n % n_chunks == 0)
    def _():
        q = jnp.concatenate([q_ref[...]] * N_HEADS, axis=0)
        qr = lax.broadcasted_iota(jnp.int32, (hq, d), 0)
        qc = lax.broadcasted_iota(jnp.int32, (hq, d), 1)
        qbd_ref[...] = jnp.where(qr // ts == qc // HEAD_DIM, q, 0.0).astype(BF16)
        r_ref[...] = jnp.zeros_like(r_ref)
        acc_ref[...] = jnp.zeros_like(acc_ref)
        pad_ref[...] = jnp.zeros_like(pad_ref)
        pad_ref[0, 0:ts, :] = kn_ref[...]
        pad_ref[1, 0:ts, :] = vn_ref[...]
        qry = lax.broadcasted_iota(jnp.int32, (hq, PAGE_SIZE), 0) % ts
        key = lax.broadcasted_iota(jnp.int32, (hq, PAGE_SIZE), 1)
        z = _dot_nt(qbd_ref[...], pad_ref[0].astype(BF16)) + bias_ref[...]
        fold(z, key < qry, lambda a: _dot(a, pad_ref[1].astype(BF16)))

    for cp in _page_copies(n, slot, pt_ref, kpool_ref, vpool_ref, kbuf_ref, vbuf_ref, sem_ref,
                           n_chunks=n_chunks, pages=pages):
        cp.wait()
    all_pages = lambda buf: jnp.concatenate([buf[slot, p] for p in range(pages)], axis=1).astype(BF16)
    z = _dot(qbd_ref[...], all_pages(kbuf_ref)) + bias_ref[...]
    fold(z, None, lambda a: _dot_nt(a, all_pages(vbuf_ref)))

    @pl.when(n % n_chunks == n_chunks - 1)
    def _():
        orow = lax.broadcasted_iota(jnp.int32, (hq, d), 0)
        ocol = lax.broadcasted_iota(jnp.int32, (hq, d), 1)
        o = jnp.where(orow // ts == ocol // HEAD_DIM, acc_ref[...], 0.0)
        o_ref[...] = jnp.sum(o.reshape(N_HEADS, ts, d), axis=0)


def _sb_attention_kernel(pt_ref, pbias_ref, qp_ref, kp_ref, vp_ref, sbias_ref, qs_ref, kn_ref, vn_ref,
                         kpool_ref, vpool_ref, op_ref, os_ref,
                         qq_ref, tri_ref, u_ref, h_ref, rs_ref, r_ref, acc_ref,
                         kbuf_ref, vbuf_ref, pad_ref, qbd_ref, sr_ref, sacc_ref, sem_ref,
                         *, tq, tk, n_chunks, pages, n_sample_steps):
    hp, qi = pl.program_id(1), pl.program_id(2)
    n = (pl.program_id(0) * pl.num_programs(1) + hp) * pl.num_programs(2) + qi
    _sb_sample_prefetch(n, n_sample_steps, pt_ref, kpool_ref, vpool_ref, kbuf_ref, vbuf_ref, sem_ref,
                        n_chunks=n_chunks, pages=pages)
    _sb_prompt_step(hp, qi, pbias_ref, qp_ref, kp_ref, vp_ref, op_ref,
                    qq_ref, tri_ref, u_ref, h_ref, rs_ref, r_ref, acc_ref, tq=tq, tk=tk)

    @pl.when(n < n_sample_steps)
    def _():
        _sb_sample_step(n, pt_ref, sbias_ref, qs_ref, kn_ref, vn_ref, kpool_ref, vpool_ref, os_ref,
                        kbuf_ref, vbuf_ref, pad_ref, qbd_ref, sr_ref, sacc_ref, sem_ref, n_chunks=n_chunks, pages=pages)


def _sb_attention(qp, kp, vp, qs, k_new, v_new, k_pool, v_pool, page_table, bias, *, tq, tk, pages):
    b, t, d = qp.shape
    nb, ts, _ = qs.shape
    n_pages = page_table.shape[1]
    assert t % tq == 0 and tq == 2 * tk and tk % LANES == 0
    assert d == N_HEADS * HEAD_DIM and 2 * HEAD_DIM == LANES
    assert ts == SUBLANES and N_HEADS * ts == LANES and n_pages % pages == 0 and (pages * PAGE_SIZE) % MXU_DIM == 0
    assert k_pool.shape[1:] == (PAGE_SIZE, N_HEADS, HEAD_DIM)
    n_chunks = n_pages // pages
    n_pool = k_pool.shape[0]
    hq = N_HEADS * ts
    hps, nq = N_HEADS // 2, t // tq
    n_sample_steps = nb * n_chunks
    assert n_sample_steps <= b * hps * nq, "more sample chunks than prompt grid steps"
    pool_t = lambda a: jnp.transpose(a, (0, 2, 3, 1)).reshape(n_pool, d, PAGE_SIZE)
    kv_spec = pl.BlockSpec((None, t, LANES), lambda i, h, j, pt: (i, 0, h))
    q_spec = pl.BlockSpec((None, tq, LANES), lambda i, h, j, pt: (i, j, h))
    tok = pl.BlockSpec((None, ts, d), lambda i, h, j, pt: (jnp.minimum(((i * hps + h) * nq + j) // n_chunks, nb - 1), 0, 0))
    m = 2 * tq
    grid_spec = pltpu.PrefetchScalarGridSpec(
        num_scalar_prefetch=1,
        grid=(b, hps, nq),
        in_specs=[pl.BlockSpec(memory_space=pltpu.SMEM), q_spec, kv_spec, kv_spec,
                  pl.BlockSpec((hq, 1), lambda i, h, j, pt: (0, 0)), tok, tok, tok,
                  pl.BlockSpec(memory_space=pl.ANY), pl.BlockSpec(memory_space=pl.ANY)],
        out_specs=[q_spec, tok],
        scratch_shapes=[pltpu.VMEM((m, LANES), BF16), pltpu.VMEM((tk, tk), BF16),
                        pltpu.VMEM((2, m, tk), F32), pltpu.VMEM((2, m, tk), BF16),
                        pltpu.VMEM((2, m, LANES), F32), pltpu.VMEM((m, LANES), F32), pltpu.VMEM((m, LANES), F32),
                        pltpu.VMEM((2, pages, d, PAGE_SIZE), F32), pltpu.VMEM((2, pages, d, PAGE_SIZE), F32),
                        pltpu.VMEM((2, PAGE_SIZE, d), F32), pltpu.VMEM((hq, d), BF16),
                        pltpu.VMEM((hq, 1), F32), pltpu.VMEM((hq, d), F32),
                        pltpu.SemaphoreType.DMA((2, 2))],
    )
    return pl.pallas_call(
        functools.partial(_sb_attention_kernel, tq=tq, tk=tk, n_chunks=n_chunks, pages=pages,
                          n_sample_steps=n_sample_steps),
        grid_spec=grid_spec,
        out_shape=[jax.ShapeDtypeStruct((b, t, d), BF16), jax.ShapeDtypeStruct((nb, ts, d), F32)],
        compiler_params=pltpu.CompilerParams(dimension_semantics=("arbitrary", "arbitrary", "arbitrary"),
                                             vmem_limit_bytes=VMEM_LIMIT),
        name="sb_attention",
    )(page_table.reshape(-1), bias.astype(F32), qp, kp, vp,
      jnp.repeat(bias.astype(F32), ts).reshape(hq, 1), qs, k_new, v_new, pool_t(k_pool), pool_t(v_pool))


def kernel(x_prompt, x_sample, cache_conv, cache_k, cache_v, page_table, mix_norm_g, ffn_norm_g, final_norm_g,
           cv_w_pw1, cv_b_pw1, cv_w_dw, cv_b_dw, cv_ln_g, cv_ln_b, cv_w_pw2, cv_b_pw2,
           sb_w_qkv, sb_w_o, sb_logit_bias, ffn_w_gate, ffn_w_up, ffn_w_down):
    bp, tp, d = x_prompt.shape
    bs, ts, _ = x_sample.shape
    assert mix_norm_g.shape[0] == 2 and tp % PAGE_SIZE == 0

    cw = _conv_weights(mix_norm_g[0], cv_w_pw1[0], cv_b_pw1[0], cv_w_dw[0], cv_b_dw[0], cv_ln_g[0], cv_ln_b[0],
                       cv_w_pw2[0], cv_b_pw2[0])
    xp, st_p = _conv_prompt(x_prompt, cw, tile_t=256)
    xs, st_s = _conv_sample(x_sample, cache_conv[0], cw)
    ffn0 = (ffn_norm_g[0], ffn_w_gate[0], ffn_w_up[0], ffn_w_down[0])
    xp = _ffn(xp.reshape(bp * tp, d), *ffn0, tile_m=512)
    xs = _ffn(xs.reshape(bs * ts, d), *ffn0, tile_m=512)

    qp, kpb, vpb, kp_t, vp_t = _qkv(xp, mix_norm_g[1], sb_w_qkv[0], tile_m=512, paged=True)
    qs, ks, vs = _qkv(xs, mix_norm_g[1], sb_w_qkv[0], tile_m=512, paged=False)
    seq = lambda a: a.reshape(bp, tp, d)
    dec = lambda a: a.reshape(bs, ts, d)
    op, os_ = _sb_attention(seq(qp), seq(kpb), seq(vpb), dec(qs), dec(ks), dec(vs), cache_k[0], cache_v[0],
                            page_table, sb_logit_bias[0], tq=512, tk=256, pages=16)
    ffn1 = (ffn_norm_g[1], ffn_w_gate[1], ffn_w_up[1], ffn_w_down[1])
    yp = _ffn(xp, *ffn1, tile_m=512, proj=(op.reshape(bp * tp, d), sb_w_o[0]), final_g=final_norm_g)
    ys = _ffn(xs, *ffn1, tile_m=512, proj=(os_.reshape(bs * ts, d), sb_w_o[0]), final_g=final_norm_g)

    pages_p = lambda a: jnp.transpose(a.reshape(1, bp, tp // PAGE_SIZE, N_HEADS, HEAD_DIM, PAGE_SIZE), (0, 1, 2, 5, 3, 4))
    new_s = (1, bs, ts, N_HEADS, HEAD_DIM)
    return (yp.reshape(bp, tp, d), ys.reshape(bs, ts, d), st_p[None], st_s[None],
            pages_p(kp_t), pages_p(vp_t), ks.reshape(new_s), vs.reshape(new_s))
```

```python
import functools

import jax
import jax.numpy as jnp
from jax import lax
from jax.experimental import pallas as pl
from jax.experimental.pallas import tpu as pltpu

N_HEADS = 16
HEAD_DIM = 64
CONV_WIDTH = 31
CONV_STATE = CONV_WIDTH - 1
PAGE_SIZE = 128
RMS_EPS = 1e-6
LN_EPS = 1e-5
LOG2_E = 1.4426950408889634

F32 = jnp.float32
BF16 = jnp.bfloat16

LANES = 128
SUBLANES = 8
MXU_DIM = 256
HALO = 32
VMEM_LIMIT = 56 * 1024 * 1024


def _resident(shape):
    nd = len(shape)
    return pl.BlockSpec(shape, lambda *_: (0,) * nd, pipeline_mode=pl.Buffered(1))


def _rmsnorm(x, g):
    return x * lax.rsqrt(jnp.mean(x * x, axis=-1, keepdims=True) + RMS_EPS) * g


def _sigmoid(x):
    return 1.0 / (1.0 + jnp.exp(-x))


def _dot(a, b):
    return jnp.dot(a, b, preferred_element_type=F32)


def _dot_nt(a, b):
    return lax.dot_general(a, b, (((1,), (1,)), ((), ())), preferred_element_type=F32)


def _softplus(z):
    return jnp.maximum(z, 0.0) + jnp.log(1.0 + jnp.exp2(jnp.abs(z) * -LOG2_E))


def _glu(h, w1_ref, b1_ref, d):
    a = _dot(h, w1_ref[:, :d]) + b1_ref[:, :d]
    gate = _dot(h, w1_ref[:, d:]) + b1_ref[:, d:]
    return a * _sigmoid(gate)


def _ln_silu_pw2(c, lng_ref, lnb_ref, w2_ref, b2_ref):
    mu = jnp.mean(c, axis=-1, keepdims=True)
    cc = c - mu
    var = jnp.mean(cc * cc, axis=-1, keepdims=True)
    y = cc * lax.rsqrt(var + LN_EPS) * lng_ref[...] + lnb_ref[...]
    y = y * _sigmoid(y)
    return _dot(y.astype(BF16), w2_ref[...]) + b2_ref[...]


def _conv_prompt_kernel(x_ref, g_ref, w1_ref, b1_ref, wdw_ref, bdw_ref, lng_ref, lnb_ref, w2_ref, b2_ref,
                        y_ref, st_ref, ubuf_ref, cbuf_ref, *, rows, cols):
    tt, d = x_ref.shape

    @pl.when(pl.program_id(1) == 0)
    def _():
        ubuf_ref[0:HALO, :] = jnp.zeros((HALO, d), F32)

    x = x_ref[...]
    h = _rmsnorm(x, g_ref[...]).astype(BF16)
    ubuf_ref[HALO:HALO + tt, :] = _glu(h, w1_ref, b1_ref, d)

    off = HALO - CONV_STATE
    for c0 in range(0, d, cols):
        for r0 in range(0, tt, rows):
            acc = jnp.broadcast_to(bdw_ref[:, c0:c0 + cols], (rows, cols))
            for phase in range(SUBLANES):
                taps = [m for m in range(phase, HALO + 1, SUBLANES) if 0 <= m - off < CONV_WIDTH]
                if not taps:
                    continue
                slab = ubuf_ref[r0 + phase:r0 + phase + rows + taps[-1] - phase, c0:c0 + cols]
                for m in taps:
                    w = wdw_ref[m - off, :, c0:c0 + cols]
                    seg = slab[m - phase:m - phase + rows].reshape(rows // SUBLANES, SUBLANES, cols)
                    acc = acc + (seg * w[None]).reshape(rows, cols)
            cbuf_ref[r0:r0 + rows, c0:c0 + cols] = acc

    y_ref[...] = x + _ln_silu_pw2(cbuf_ref[...], lng_ref, lnb_ref, w2_ref, b2_ref)
    st_ref[...] = ubuf_ref[tt + off:tt + HALO, :]
    ubuf_ref[0:HALO, :] = ubuf_ref[tt:tt + HALO, :]


def _conv_sample_kernel(x_ref, cache_ref, g_ref, w1_ref, b1_ref, wdw_ref, bdw_ref, lng_ref, lnb_ref, w2_ref, b2_ref,
                        y_ref, st_ref, ext_ref, cbuf_ref, *, cols):
    n, d = x_ref.shape
    nb, ts = ext_ref.shape[0], ext_ref.shape[1] - HALO
    off = HALO - CONV_STATE
    x = x_ref[...]
    h = _rmsnorm(x, g_ref[...]).astype(BF16)
    ext_ref[:, off:HALO, :] = cache_ref[...]
    ext_ref[:, HALO:, :] = _glu(h, w1_ref, b1_ref, d).reshape(nb, ts, d)
    for c0 in range(0, d, cols):
        acc = jnp.broadcast_to(bdw_ref[:, c0:c0 + cols][None], (nb, ts, cols))
        for k in range(CONV_WIDTH):
            acc = acc + ext_ref[:, off + k:off + k + ts, c0:c0 + cols] * wdw_ref[k, :, c0:c0 + cols][None]
        cbuf_ref[:, c0:c0 + cols] = acc.reshape(n, cols)
    y_ref[...] = x + _ln_silu_pw2(cbuf_ref[...], lng_ref, lnb_ref, w2_ref, b2_ref)
    st_ref[...] = ext_ref[:, ts + off:, :]


def _conv_weights(g, w1, b1, wdw, bdw, lng, lnb, w2, b2):
    d = w2.shape[0]
    row = lambda v: v.reshape(1, -1).astype(F32)
    wdw8 = jnp.broadcast_to(wdw[:, None, :], (CONV_WIDTH, SUBLANES, d))
    return (row(g), w1.astype(BF16), row(b1), wdw8, row(bdw), row(lng), row(lnb), w2.astype(BF16), row(b2))


def _conv_weight_specs(d):
    return [_resident((1, d)), _resident((d, 2 * d)), _resident((1, 2 * d)), _resident((CONV_WIDTH, SUBLANES, d)),
            _resident((1, d)), _resident((1, d)), _resident((1, d)), _resident((d, d)), _resident((1, d))]


def _conv_prompt(x, weights, *, tile_t):
    b, t, d = x.shape
    tt = min(tile_t, t)
    assert t % tt == 0 and tt % 64 == 0 and d % 256 == 0
    kern = functools.partial(_conv_prompt_kernel, rows=64, cols=256)
    return pl.pallas_call(
        kern,
        grid=(b, t // tt),
        in_specs=[pl.BlockSpec((None, tt, d), lambda i, j: (i, j, 0))] + _conv_weight_specs(d),
        out_specs=[pl.BlockSpec((None, tt, d), lambda i, j: (i, j, 0)),
                   pl.BlockSpec((None, CONV_STATE, d), lambda i, j: (i, 0, 0))],
        out_shape=[jax.ShapeDtypeStruct((b, t, d), F32), jax.ShapeDtypeStruct((b, CONV_STATE, d), F32)],
        scratch_shapes=[pltpu.VMEM((HALO + tt, d), F32), pltpu.VMEM((tt, d), F32)],
        compiler_params=pltpu.CompilerParams(dimension_semantics=("arbitrary", "arbitrary"),
                                             vmem_limit_bytes=VMEM_LIMIT),
        name="conv_prompt",
    )(x, *weights)


def _conv_sample(x, cache, weights):
    nb, ts, d = x.shape
    assert ts == SUBLANES and cache.shape == (nb, CONV_STATE, d)
    n = nb * ts
    kern = functools.partial(_conv_sample_kernel, cols=128)
    y, st = pl.pallas_call(
        kern,
        grid=(1,),
        in_specs=[_resident((n, d)), _resident((nb, CONV_STATE, d))] + _conv_weight_specs(d),
        out_specs=[pl.BlockSpec((n, d), lambda i: (0, 0)), pl.BlockSpec((nb, CONV_STATE, d), lambda i: (0, 0, 0))],
        out_shape=[jax.ShapeDtypeStruct((n, d), F32), jax.ShapeDtypeStruct((nb, CONV_STATE, d), F32)],
        scratch_shapes=[pltpu.VMEM((nb, HALO + ts, d), F32), pltpu.VMEM((n, d), F32)],
        compiler_params=pltpu.CompilerParams(dimension_semantics=("arbitrary",), vmem_limit_bytes=VMEM_LIMIT),
        name="conv_sample",
    )(x.reshape(n, d), cache, *weights)
    return y.reshape(nb, ts, d), st


def _ffn_kernel(*refs, has_proj, has_final, ff_chunk):
    refs = list(refs)
    x_ref = refs.pop(0)
    if has_proj:
        o_ref, wo_ref = refs.pop(0), refs.pop(0)
    g_ref, wg_ref, wu_ref, wd_ref = refs.pop(0), refs.pop(0), refs.pop(0), refs.pop(0)
    if has_final:
        gf_ref = refs.pop(0)
    y_ref, h_ref = refs
    x = x_ref[...]
    if has_proj:
        x = x + _dot(o_ref[...].astype(BF16), wo_ref[...])
    h_ref[...] = _rmsnorm(x, g_ref[...]).astype(BF16)
    acc = x
    d_ff = wg_ref.shape[1]
    for c0 in range(0, d_ff, ff_chunk):
        h = h_ref[...]
        gate = _dot(h, wg_ref[:, c0:c0 + ff_chunk])
        up = _dot(h, wu_ref[:, c0:c0 + ff_chunk])
        act = (gate * _sigmoid(gate) * up).astype(BF16)
        acc = acc + _dot(act, wd_ref[c0:c0 + ff_chunk, :])
    if has_final:
        acc = _rmsnorm(acc, gf_ref[...])
    y_ref[...] = acc


def _ffn(x, g, wg, wu, wd, layer, *, tile_m, proj=None, final_g=None):
    n, d = x.shape
    d_ff = wg.shape[2]
    tm = min(tile_m, n)
    assert n % tm == 0 and d_ff % MXU_DIM == 0
    tok = lambda last: pl.BlockSpec((tm, last), lambda i: (i, 0))
    of_layer = lambda r, c: pl.BlockSpec((None, r, c), lambda i: (layer, 0, 0), pipeline_mode=pl.Buffered(1))
    args, specs = [x], [tok(d)]
    if proj is not None:
        o, wo = proj
        args += [o, wo.astype(BF16)]
        specs += [tok(o.shape[1]), _resident(wo.shape)]
    args += [g.reshape(1, d), wg, wu, wd]
    specs += [_resident((1, d)), of_layer(d, d_ff), of_layer(d, d_ff), of_layer(d_ff, d)]
    if final_g is not None:
        args.append(final_g.reshape(1, d))
        specs.append(_resident((1, d)))
    kern = functools.partial(_ffn_kernel, has_proj=proj is not None, has_final=final_g is not None, ff_chunk=MXU_DIM)
    return pl.pallas_call(
        kern,
        grid=(n // tm,),
        in_specs=specs,
        out_specs=tok(d),
        out_shape=jax.ShapeDtypeStruct((n, d), F32),
        scratch_shapes=[pltpu.VMEM((tm, d), BF16)],
        compiler_params=pltpu.CompilerParams(dimension_semantics=("parallel",), vmem_limit_bytes=VMEM_LIMIT),
        name="ffn",
    )(*args)


def _qkv_kernel(x_ref, g_ref, w_ref, *rest, paged):
    tm, d = x_ref.shape
    h = _rmsnorm(x_ref[...], g_ref[...]).astype(BF16)
    q = _dot(h, w_ref[:, :d]) * (HEAD_DIM ** -0.5)
    k = _dot(h, w_ref[:, d:2 * d])
    v = _dot(h, w_ref[:, 2 * d:])
    if paged:
        q_ref, kb_ref, vb_ref, kt_ref, vt_ref = rest
        kb_ref[...] = k.astype(BF16)
        vb_ref[...] = v.astype(BF16)
        for t_ref, x in ((kt_ref, k), (vt_ref, v)):
            xt = x.T
            for p in range(tm // PAGE_SIZE):
                t_ref[p] = xt[:, p * PAGE_SIZE:(p + 1) * PAGE_SIZE]
    else:
        q_ref, k_ref, v_ref = rest
        k_ref[...] = k
        v_ref[...] = v
    q_ref[...] = q.astype(q_ref.dtype)


def _qkv(x, g, w_qkv, *, tile_m, paged):
    n, d = x.shape
    assert w_qkv.shape == (d, 3 * d) and d == N_HEADS * HEAD_DIM
    tm = min(tile_m, n)
    assert n % tm == 0
    tok = pl.BlockSpec((tm, d), lambda i: (i, 0))
    f32o, bf16o = jax.ShapeDtypeStruct((n, d), F32), jax.ShapeDtypeStruct((n, d), BF16)
    args = [x, g.reshape(1, d), w_qkv.astype(BF16)]
    in_specs = [tok, _resident((1, d)), _resident((d, 3 * d))]
    if paged:
        assert tm % PAGE_SIZE == 0
        pages = pl.BlockSpec((tm // PAGE_SIZE, d, PAGE_SIZE), lambda i: (i, 0, 0))
        paged_o = jax.ShapeDtypeStruct((n // PAGE_SIZE, d, PAGE_SIZE), F32)
        out_shape, out_specs = [bf16o, bf16o, bf16o, paged_o, paged_o], [tok, tok, tok, pages, pages]
    else:
        out_shape, out_specs = [f32o, f32o, f32o], [tok, tok, tok]
    return pl.pallas_call(
        functools.partial(_qkv_kernel, paged=paged),
        grid=(n // tm,),
        in_specs=in_specs,
        out_specs=out_specs,
        out_shape=out_shape,
        compiler_params=pltpu.CompilerParams(dimension_semantics=("parallel",), vmem_limit_bytes=VMEM_LIMIT),
        name="qkv",
    )(*args)


def _sb_prompt_step(hp, qi, bias_ref, q_ref, k_ref, v_ref, o_ref,
                    qq_ref, tri_ref, u_ref, h_ref, rs_ref, r_ref, acc_ref, *, tq, tk):
    lane = lax.broadcasted_iota(jnp.int32, (1, LANES), 1)
    zero = jnp.zeros((tk, LANES), BF16)
    for c in range(4):
        qc = q_ref[(c // 2) * tk:(c // 2 + 1) * tk, :]
        qq_ref[c * tk:(c + 1) * tk] = jnp.where((lane >= HEAD_DIM) if c % 2 else (lane < HEAD_DIM), qc, zero)
    tri_ref[...] = (lax.broadcasted_iota(jnp.int32, (tk, tk), 0)
                    > lax.broadcasted_iota(jnp.int32, (tk, tk), 1)).astype(BF16)
    r_ref[...] = jnp.zeros_like(r_ref)
    acc_ref[...] = jnp.zeros_like(acc_ref)
    bias = (bias_ref[2 * hp], bias_ref[2 * hp + 1])

    def stage_a(kb, slot, masked, r0=0):
        start = pl.multiple_of(kb * tk, tk)
        z = _dot_nt(qq_ref[r0:], k_ref[pl.ds(start, tk), :])
        for c in range(r0 // tk, 4):
            rows = slice(c * tk, (c + 1) * tk)
            zc = z[c * tk - r0:(c + 1) * tk - r0] + bias[c % 2]
            sp = _softplus(zc)
            u = zc - sp
            if masked:
                q_pos = qi * tq + (c // 2) * tk + lax.broadcasted_iota(jnp.int32, (tk, tk), 0)
                k_pos = kb * tk + lax.broadcasted_iota(jnp.int32, (tk, tk), 1)
                causal = k_pos < q_pos
                sp = jnp.where(causal, sp, 0.0)
                u = jnp.where(causal, u, -jnp.inf)
            u_ref[slot, rows] = u
            h_ref[slot, rows] = sp.astype(BF16)
            rs_ref[slot, rows] = jnp.broadcast_to(jnp.sum(sp, axis=-1, keepdims=True), (tk, LANES))

    def stage_b(kb, slot, r0=0):
        start = pl.multiple_of(kb * tk, tk)
        after = _dot(h_ref[slot, r0:], tri_ref[...])
        r = r_ref[r0:]
        e = u_ref[slot, r0:] - after - jnp.concatenate([r] * (tk // LANES), axis=1)
        acc_ref[r0:] += _dot(jnp.exp(e).astype(BF16), v_ref[pl.ds(start, tk), :])
        r_ref[r0:] = r + rs_ref[slot, r0:]

    top = 2 * qi + 1
    stage_a(top, 0, True, r0=tq)
    stage_a(top - 1, 1, True)
    stage_b(top, 0, r0=tq)

    def pair(kb):
        stage_a(kb, 0, False)
        stage_b(kb + 1, 1)
        stage_a(kb - 1, 1, False)
        stage_b(kb, 0)

    odd = qi % 2

    @pl.when(odd == 1)
    def _():
        pair(2 * qi - 1)

    def quad(t, carry):
        kb = 2 * (qi - odd) - 1 - 4 * t
        pair(kb)
        pair(kb - 2)
        return carry

    lax.fori_loop(0, qi // 2, quad, 0)
    stage_b(0, 1)
    for half in range(2):
        lo, hi = acc_ref[2 * half * tk:(2 * half + 1) * tk], acc_ref[(2 * half + 1) * tk:(2 * half + 2) * tk]
        o_ref[half * tk:(half + 1) * tk, :] = jnp.where(lane < HEAD_DIM, lo, hi).astype(o_ref.dtype)


def _page_copies(step, slot, pt_ref, kpool_ref, vpool_ref, kbuf_ref, vbuf_ref, sem_ref, *, n_chunks, pages):
    n_pages = n_chunks * pages
    seq = step // n_chunks
    first = (n_chunks - 1 - step % n_chunks) * pages
    out = []
    for p in range(pages):
        page = pt_ref[seq * n_pages + first + p]
        out.append(pltpu.make_async_copy(kpool_ref.at[page], kbuf_ref.at[slot, p], sem_ref.at[0, slot]))
        out.append(pltpu.make_async_copy(vpool_ref.at[page], vbuf_ref.at[slot, p], sem_ref.at[1, slot]))
    return out


def _sb_sample_prefetch(n, n_steps, *dma_refs, n_chunks, pages):
    copies = functools.partial(_page_copies, n_chunks=n_chunks, pages=pages)

    @pl.when(n == 0)
    def _():
        for cp in copies(n, n % 2, *dma_refs):
            cp.start()

    @pl.when(n + 1 < n_steps)
    def _():
        for cp in copies(n + 1, 1 - n % 2, *dma_refs):
            cp.start()


def _sb_sample_step(n, pt_ref, bias_ref, q_ref, kn_ref, vn_ref, kpool_ref, vpool_ref, o_ref,
                    kbuf_ref, vbuf_ref, pad_ref, qbd_ref, r_ref, acc_ref, sem_ref, *, n_chunks, pages):
    ts, d = q_ref.shape
    hq = N_HEADS * ts
    blk = MXU_DIM
    slot = n % 2
    row = lax.broadcasted_iota(jnp.int32, (blk, blk), 0)
    col = lax.broadcasted_iota(jnp.int32, (blk, blk), 1)
    tri = (row > col).astype(BF16)

    def fold(z, valid, times_v):
        w = z.shape[1]
        step = min(blk, w)
        sp = _softplus(z)
        u = z - sp
        if valid is not None:
            sp = jnp.where(valid, sp, 0.0)
            u = jnp.where(valid, u, -jnp.inf)
        r = r_ref[...]
        es = [None] * (w // step)
        for j in reversed(range(w // step)):
            spj = sp[:, j * step:(j + 1) * step]
            after = _dot(spj.astype(BF16), tri[:step, :step])
            es[j] = u[:, j * step:(j + 1) * step] - after - r
            r = r + jnp.sum(spj, axis=-1, keepdims=True)
        acc_ref[...] += times_v(jnp.exp(jnp.concatenate(es, axis=1)).astype(BF16))
        r_ref[...] = r

    @pl.when(n % n_chunks == 0)
    def _():
        q = jnp.concatenate([q_ref[...]] * N_HEADS, axis=0)
        qr = lax.broadcasted_iota(jnp.int32, (hq, d), 0)
        qc = lax.broadcasted_iota(jnp.int32, (hq, d), 1)
        qbd_ref[...] = jnp.where(qr // ts == qc // HEAD_DIM, q, 0.0).astype(BF16)
        r_ref[...] = jnp.zeros_like(r_ref)
        acc_ref[...] = jnp.zeros_like(acc_ref)
        pad_ref[...] = jnp.zeros_like(pad_ref)
        pad_ref[0, 0:ts, :] = kn_ref[...]
        pad_ref[1, 0:ts, :] = vn_ref[...]
        qry = lax.broadcasted_iota(jnp.int32, (hq, PAGE_SIZE), 0) % ts
        key = lax.broadcasted_iota(jnp.int32, (hq, PAGE_SIZE), 1)
        z = _dot_nt(qbd_ref[...], pad_ref[0].astype(BF16)) + bias_ref[...]
        fold(z, key < qry, lambda a: _dot(a, pad_ref[1].astype(BF16)))

    for cp in _page_copies(n, slot, pt_ref, kpool_ref, vpool_ref, kbuf_ref, vbuf_ref, sem_ref,
                           n_chunks=n_chunks, pages=pages):
        cp.wait()
    all_pages = lambda buf: jnp.concatenate([buf[slot, p] for p in range(pages)], axis=1).astype(BF16)
    z = _dot(qbd_ref[...], all_pages(kbuf_ref)) + bias_ref[...]
    fold(z, None, lambda a: _dot_nt(a, all_pages(vbuf_ref)))

    @pl.when(n % n_chunks == n_chunks - 1)
    def _():
        orow = lax.broadcasted_iota(jnp.int32, (hq, d), 0)
        ocol = lax.broadcasted_iota(jnp.int32, (hq, d), 1)
        o = jnp.where(orow // ts == ocol // HEAD_DIM, acc_ref[...], 0.0)
        o_ref[...] = jnp.sum(o.reshape(N_HEADS, ts, d), axis=0)


def _sb_attention_kernel(pt_ref, pbias_ref, qp_ref, kp_ref, vp_ref, sbias_ref, qs_ref, kn_ref, vn_ref,
                         kpool_ref, vpool_ref, op_ref, os_ref,
                         qq_ref, tri_ref, u_ref, h_ref, rs_ref, r_ref, acc_ref,
                         kbuf_ref, vbuf_ref, pad_ref, qbd_ref, sr_ref, sacc_ref, sem_ref,
                         *, tq, tk, n_chunks, pages, n_sample_steps):
    hp, qi = pl.program_id(1), pl.program_id(2)
    n = (pl.program_id(0) * pl.num_programs(1) + hp) * pl.num_programs(2) + qi
    _sb_sample_prefetch(n, n_sample_steps, pt_ref, kpool_ref, vpool_ref, kbuf_ref, vbuf_ref, sem_ref,
                        n_chunks=n_chunks, pages=pages)
    _sb_prompt_step(hp, qi, pbias_ref, qp_ref, kp_ref, vp_ref, op_ref,
                    qq_ref, tri_ref, u_ref, h_ref, rs_ref, r_ref, acc_ref, tq=tq, tk=tk)

    @pl.when(n < n_sample_steps)
    def _():
        _sb_sample_step(n, pt_ref, sbias_ref, qs_ref, kn_ref, vn_ref, kpool_ref, vpool_ref, os_ref,
                        kbuf_ref, vbuf_ref, pad_ref, qbd_ref, sr_ref, sacc_ref, sem_ref, n_chunks=n_chunks, pages=pages)


def _sb_attention(qp, kp, vp, qs, k_new, v_new, k_pool, v_pool, page_table, bias, *, tq, tk, pages):
    b, t, d = qp.shape
    nb, ts, _ = qs.shape
    n_pages = page_table.shape[1]
    assert t % tq == 0 and tq == 2 * tk and tk % LANES == 0
    assert d == N_HEADS * HEAD_DIM and 2 * HEAD_DIM == LANES
    assert ts == SUBLANES and N_HEADS * ts == LANES and n_pages % pages == 0 and (pages * PAGE_SIZE) % MXU_DIM == 0
    assert k_pool.shape[1:] == (PAGE_SIZE, N_HEADS, HEAD_DIM)
    n_chunks = n_pages // pages
    n_pool = k_pool.shape[0]
    hq = N_HEADS * ts
    hps, nq = N_HEADS // 2, t // tq
    n_sample_steps = nb * n_chunks
    assert n_sample_steps <= b * hps * nq, "more sample chunks than prompt grid steps"
    pool_t = lambda a: jnp.transpose(a, (0, 2, 3, 1)).reshape(n_pool, d, PAGE_SIZE)
    kv_spec = pl.BlockSpec((None, t, LANES), lambda i, h, j, pt: (i, 0, h))
    q_spec = pl.BlockSpec((None, tq, LANES), lambda i, h, j, pt: (i, j, h))
    tok = pl.BlockSpec((None, ts, d), lambda i, h, j, pt: (jnp.minimum(((i * hps + h) * nq + j) // n_chunks, nb - 1), 0, 0))
    m = 2 * tq
    grid_spec = pltpu.PrefetchScalarGridSpec(
        num_scalar_prefetch=1,
        grid=(b, hps, nq),
        in_specs=[pl.BlockSpec(memory_space=pltpu.SMEM), q_spec, kv_spec, kv_spec,
                  pl.BlockSpec((hq, 1), lambda i, h, j, pt: (0, 0)), tok, tok, tok,
                  pl.BlockSpec(memory_space=pl.ANY), pl.BlockSpec(memory_space=pl.ANY)],
        out_specs=[q_spec, tok],
        scratch_shapes=[pltpu.VMEM((m, LANES), BF16), pltpu.VMEM((tk, tk), BF16),
                        pltpu.VMEM((2, m, tk), F32), pltpu.VMEM((2, m, tk), BF16),
                        pltpu.VMEM((2, m, LANES), F32), pltpu.VMEM((m, LANES), F32), pltpu.VMEM((m, LANES), F32),
                        pltpu.VMEM((2, pages, d, PAGE_SIZE), F32), pltpu.VMEM((2, pages, d, PAGE_SIZE), F32),
                        pltpu.VMEM((2, PAGE_SIZE, d), F32), pltpu.VMEM((hq, d), BF16),
                        pltpu.VMEM((hq, 1), F32), pltpu.VMEM((hq, d), F32),
                        pltpu.SemaphoreType.DMA((2, 2))],
    )
    return pl.pallas_call(
        functools.partial(_sb_attention_kernel, tq=tq, tk=tk, n_chunks=n_chunks, pages=pages,
                          n_sample_steps=n_sample_steps),
        grid_spec=grid_spec,
        out_shape=[jax.ShapeDtypeStruct((b, t, d), BF16), jax.ShapeDtypeStruct((nb, ts, d), F32)],
        compiler_params=pltpu.CompilerParams(dimension_semantics=("arbitrary", "arbitrary", "arbitrary"),
                                             vmem_limit_bytes=VMEM_LIMIT),
        name="sb_attention",
    )(page_table.reshape(-1), bias.astype(F32), qp, kp, vp,
      jnp.repeat(bias.astype(F32), ts).reshape(hq, 1), qs, k_new, v_new, pool_t(k_pool), pool_t(v_pool))


def kernel(x_prompt, x_sample, cache_conv, cache_k, cache_v, page_table, mix_norm_g, ffn_norm_g, final_norm_g,
           cv_w_pw1, cv_b_pw1, cv_w_dw, cv_b_dw, cv_ln_g, cv_ln_b, cv_w_pw2, cv_b_pw2,
           sb_w_qkv, sb_w_o, sb_logit_bias, ffn_w_gate, ffn_w_up, ffn_w_down):
    bp, tp, d = x_prompt.shape
    bs, ts, _ = x_sample.shape
    assert mix_norm_g.shape[0] == 2 and tp % PAGE_SIZE == 0

    cw = _conv_weights(mix_norm_g[0], cv_w_pw1[0], cv_b_pw1[0], cv_w_dw[0], cv_b_dw[0], cv_ln_g[0], cv_ln_b[0],
                       cv_w_pw2[0], cv_b_pw2[0])
    xp, st_p = _conv_prompt(x_prompt, cw, tile_t=256)
    xs, st_s = _conv_sample(x_sample, cache_conv[0], cw)
    ffn_w = (ffn_w_gate.astype(BF16), ffn_w_up.astype(BF16), ffn_w_down.astype(BF16))
    xp = _ffn(xp.reshape(bp * tp, d), ffn_norm_g[0], *ffn_w, 0, tile_m=512)
    xs = _ffn(xs.reshape(bs * ts, d), ffn_norm_g[0], *ffn_w, 0, tile_m=512)

    qp, kpb, vpb, kp_t, vp_t = _qkv(xp, mix_norm_g[1], sb_w_qkv[0], tile_m=512, paged=True)
    qs, ks, vs = _qkv(xs, mix_norm_g[1], sb_w_qkv[0], tile_m=512, paged=False)
    seq = lambda a: a.reshape(bp, tp, d)
    dec = lambda a: a.reshape(bs, ts, d)
    op, os_ = _sb_attention(seq(qp), seq(kpb), seq(vpb), dec(qs), dec(ks), dec(vs), cache_k[0], cache_v[0],
                            page_table, sb_logit_bias[0], tq=512, tk=256, pages=16)
    yp = _ffn(xp, ffn_norm_g[1], *ffn_w, 1, tile_m=512, proj=(op.reshape(bp * tp, d), sb_w_o[0]), final_g=final_norm_g)
    ys = _ffn(xs, ffn_norm_g[1], *ffn_w, 1, tile_m=512, proj=(os_.reshape(bs * ts, d), sb_w_o[0]), final_g=final_norm_g)

    pages_p = lambda a: jnp.transpose(a.reshape(1, bp, tp // PAGE_SIZE, N_HEADS, HEAD_DIM, PAGE_SIZE), (0, 1, 2, 5, 3, 4))
    new_s = (1, bs, ts, N_HEADS, HEAD_DIM)
    return (yp.reshape(bp, tp, d), ys.reshape(bs, ts, d), st_p[None], st_s[None],
            pages_p(kp_t), pages_p(vp_t), ks.reshape(new_s), vs.reshape(new_s))
```

```python
import functools

import jax
import jax.numpy as jnp
from jax import lax
from jax.experimental import pallas as pl
from jax.experimental.pallas import tpu as pltpu

N_HEADS = 16
HEAD_DIM = 64
CONV_WIDTH = 31
CONV_STATE = CONV_WIDTH - 1
PAGE_SIZE = 128
RMS_EPS = 1e-6
LN_EPS = 1e-5
LOG2_E = 1.4426950408889634

F32 = jnp.float32
BF16 = jnp.bfloat16

LANES = 128
SUBLANES = 8
MXU_DIM = 256
HALO = 32
VMEM_LIMIT = 56 * 1024 * 1024


def _resident(shape):
    nd = len(shape)
    return pl.BlockSpec(shape, lambda *_: (0,) * nd, pipeline_mode=pl.Buffered(1))


def _rmsnorm(x, g):
    return x * lax.rsqrt(jnp.mean(x * x, axis=-1, keepdims=True) + RMS_EPS) * g


def _sigmoid(x):
    return 1.0 / (1.0 + jnp.exp(-x))


def _dot(a, b):
    return jnp.dot(a, b, preferred_element_type=F32)


def _dot_nt(a, b):
    return lax.dot_general(a, b, (((1,), (1,)), ((), ())), preferred_element_type=F32)


def _softplus(z):
    return jnp.maximum(z, 0.0) + jnp.log(1.0 + jnp.exp2(jnp.abs(z) * -LOG2_E))


def _glu(h, w1_ref, b1_ref, d):
    a = _dot(h, w1_ref[:, :d]) + b1_ref[:, :d]
    gate = _dot(h, w1_ref[:, d:]) + b1_ref[:, d:]
    return a * _sigmoid(gate)


def _ln_silu_pw2(c, lng_ref, lnb_ref, w2_ref, b2_ref):
    mu = jnp.mean(c, axis=-1, keepdims=True)
    cc = c - mu
    var = jnp.mean(cc * cc, axis=-1, keepdims=True)
    y = cc * lax.rsqrt(var + LN_EPS) * lng_ref[...] + lnb_ref[...]
    y = y * _sigmoid(y)
    return _dot(y.astype(BF16), w2_ref[...]) + b2_ref[...]


def _conv_prompt_kernel(x_ref, g_ref, w1_ref, b1_ref, wdw_ref, bdw_ref, lng_ref, lnb_ref, w2_ref, b2_ref,
                        y_ref, st_ref, ubuf_ref, cbuf_ref, *, rows, cols):
    tt, d = x_ref.shape

    @pl.when(pl.program_id(1) == 0)
    def _():
        ubuf_ref[0:HALO, :] = jnp.zeros((HALO, d), F32)

    x = x_ref[...]
    h = _rmsnorm(x, g_ref[...]).astype(BF16)
    ubuf_ref[HALO:HALO + tt, :] = _glu(h, w1_ref, b1_ref, d)

    off = HALO - CONV_STATE
    for c0 in range(0, d, cols):
        for r0 in range(0, tt, rows):
            acc = jnp.broadcast_to(bdw_ref[:, c0:c0 + cols], (rows, cols))
            for phase in range(SUBLANES):
                taps = [m for m in range(phase, HALO + 1, SUBLANES) if 0 <= m - off < CONV_WIDTH]
                if not taps:
                    continue
                slab = ubuf_ref[r0 + phase:r0 + phase + rows + taps[-1] - phase, c0:c0 + cols]
                for m in taps:
                    w = wdw_ref[m - off, :, c0:c0 + cols]
                    seg = slab[m - phase:m - phase + rows].reshape(rows // SUBLANES, SUBLANES, cols)
                    acc = acc + (seg * w[None]).reshape(rows, cols)
            cbuf_ref[r0:r0 + rows, c0:c0 + cols] = acc

    y_ref[...] = x + _ln_silu_pw2(cbuf_ref[...], lng_ref, lnb_ref, w2_ref, b2_ref)
    st_ref[...] = ubuf_ref[tt + off:tt + HALO, :]
    ubuf_ref[0:HALO, :] = ubuf_ref[tt:tt + HALO, :]


def _conv_sample_kernel(x_ref, cache_ref, g_ref, w1_ref, b1_ref, wdw_ref, bdw_ref, lng_ref, lnb_ref, w2_ref, b2_ref,
                        y_ref, st_ref, ext_ref, cbuf_ref, *, cols):
    n, d = x_ref.shape
    nb, ts = ext_ref.shape[0], ext_ref.shape[1] - HALO
    off = HALO - CONV_STATE
    x = x_ref[...]
    h = _rmsnorm(x, g_ref[...]).astype(BF16)
    ext_ref[:, off:HALO, :] = cache_ref[...]
    ext_ref[:, HALO:, :] = _glu(h, w1_ref, b1_ref, d).reshape(nb, ts, d)
    for c0 in range(0, d, cols):
        acc = jnp.broadcast_to(bdw_ref[:, c0:c0 + cols][None], (nb, ts, cols))
        for k in range(CONV_WIDTH):
            acc = acc + ext_ref[:, off + k:off + k + ts, c0:c0 + cols] * wdw_ref[k, :, c0:c0 + cols][None]
        cbuf_ref[:, c0:c0 + cols] = acc.reshape(n, cols)
    y_ref[...] = x + _ln_silu_pw2(cbuf_ref[...], lng_ref, lnb_ref, w2_ref, b2_ref)
    st_ref[...] = ext_ref[:, ts + off:, :]


def _conv_weights(g, w1, b1, wdw, bdw, lng, lnb, w2, b2):
    d = w2.shape[0]
    row = lambda v: v.reshape(1, -1).astype(F32)
    wdw8 = jnp.broadcast_to(wdw[:, None, :], (CONV_WIDTH, SUBLANES, d))
    return (row(g), w1.astype(BF16), row(b1), wdw8, row(bdw), row(lng), row(lnb), w2.astype(BF16), row(b2))


def _conv_weight_specs(d):
    return [_resident((1, d)), _resident((d, 2 * d)), _resident((1, 2 * d)), _resident((CONV_WIDTH, SUBLANES, d)),
            _resident((1, d)), _resident((1, d)), _resident((1, d)), _resident((d, d)), _resident((1, d))]


def _conv_prompt(x, weights, *, tile_t):
    b, t, d = x.shape
    tt = min(tile_t, t)
    assert t % tt == 0 and tt % 64 == 0 and d % 256 == 0
    kern = functools.partial(_conv_prompt_kernel, rows=64, cols=256)
    return pl.pallas_call(
        kern,
        grid=(b, t // tt),
        in_specs=[pl.BlockSpec((None, tt, d), lambda i, j: (i, j, 0))] + _conv_weight_specs(d),
        out_specs=[pl.BlockSpec((None, tt, d), lambda i, j: (i, j, 0)),
                   pl.BlockSpec((None, CONV_STATE, d), lambda i, j: (i, 0, 0))],
        out_shape=[jax.ShapeDtypeStruct((b, t, d), F32), jax.ShapeDtypeStruct((b, CONV_STATE, d), F32)],
        scratch_shapes=[pltpu.VMEM((HALO + tt, d), F32), pltpu.VMEM((tt, d), F32)],
        compiler_params=pltpu.CompilerParams(dimension_semantics=("arbitrary", "arbitrary"),
                                             vmem_limit_bytes=VMEM_LIMIT),
        name="conv_prompt",
    )(x, *weights)


def _conv_sample(x, cache, weights):
    nb, ts, d = x.shape
    assert ts == SUBLANES and cache.shape == (nb, CONV_STATE, d)
    n = nb * ts
    kern = functools.partial(_conv_sample_kernel, cols=128)
    y, st = pl.pallas_call(
        kern,
        grid=(1,),
        in_specs=[_resident((n, d)), _resident((nb, CONV_STATE, d))] + _conv_weight_specs(d),
        out_specs=[pl.BlockSpec((n, d), lambda i: (0, 0)), pl.BlockSpec((nb, CONV_STATE, d), lambda i: (0, 0, 0))],
        out_shape=[jax.ShapeDtypeStruct((n, d), F32), jax.ShapeDtypeStruct((nb, CONV_STATE, d), F32)],
        scratch_shapes=[pltpu.VMEM((nb, HALO + ts, d), F32), pltpu.VMEM((n, d), F32)],
        compiler_params=pltpu.CompilerParams(dimension_semantics=("arbitrary",), vmem_limit_bytes=VMEM_LIMIT),
        name="conv_sample",
    )(x.reshape(n, d), cache, *weights)
    return y.reshape(nb, ts, d), st


def _ffn_kernel(*refs, has_proj, has_final, ff_chunk):
    refs = list(refs)
    x_ref = refs.pop(0)
    if has_proj:
        o_ref, wo_ref = refs.pop(0), refs.pop(0)
    g_ref, wg_ref, wu_ref, wd_ref = refs.pop(0), refs.pop(0), refs.pop(0), refs.pop(0)
    if has_final:
        gf_ref = refs.pop(0)
    y_ref, h_ref = refs
    x = x_ref[...]
    if has_proj:
        x = x + _dot(o_ref[...].astype(BF16), wo_ref[...])
    h_ref[...] = _rmsnorm(x, g_ref[...]).astype(BF16)
    acc = x
    d_ff = wg_ref.shape[1]
    for c0 in range(0, d_ff, ff_chunk):
        h = h_ref[...]
        gate = _dot(h, wg_ref[:, c0:c0 + ff_chunk])
        up = _dot(h, wu_ref[:, c0:c0 + ff_chunk])
        act = (gate * _sigmoid(gate) * up).astype(BF16)
        acc = acc + _dot(act, wd_ref[c0:c0 + ff_chunk, :])
    if has_final:
        acc = _rmsnorm(acc, gf_ref[...])
    y_ref[...] = acc


def _ffn(x, g, wg, wu, wd, layer, *, tile_m, proj=None, final_g=None):
    n, d = x.shape
    d_ff = wg.shape[2]
    tm = min(tile_m, n)
    assert n % tm == 0 and d_ff % MXU_DIM == 0
    tok = lambda last: pl.BlockSpec((tm, last), lambda i: (i, 0))
    of_layer = lambda r, c: pl.BlockSpec((None, r, c), lambda i: (layer, 0, 0), pipeline_mode=pl.Buffered(1))
    args, specs = [x], [tok(d)]
    if proj is not None:
        o, wo = proj
        args += [o, wo.astype(BF16)]
        specs += [tok(o.shape[1]), _resident(wo.shape)]
    args += [g.reshape(1, d), wg, wu, wd]
    specs += [_resident((1, d)), of_layer(d, d_ff), of_layer(d, d_ff), of_layer(d_ff, d)]
    if final_g is not None:
        args.append(final_g.reshape(1, d))
        specs.append(_resident((1, d)))
    kern = functools.partial(_ffn_kernel, has_proj=proj is not None, has_final=final_g is not None, ff_chunk=MXU_DIM)
    return pl.pallas_call(
        kern,
        grid=(n // tm,),
        in_specs=specs,
        out_specs=tok(d),
        out_shape=jax.ShapeDtypeStruct((n, d), F32),
        scratch_shapes=[pltpu.VMEM((tm, d), BF16)],
        compiler_params=pltpu.CompilerParams(dimension_semantics=("parallel",), vmem_limit_bytes=VMEM_LIMIT),
        name="ffn",
    )(*args)


def _qkv_kernel(x_ref, g_ref, w_ref, *rest, paged):
    tm, d = x_ref.shape
    h = _rmsnorm(x_ref[...], g_ref[...]).astype(BF16)
    q = _dot(h, w_ref[:, :d]) * (HEAD_DIM ** -0.5)
    k = _dot(h, w_ref[:, d:2 * d])
    v = _dot(h, w_ref[:, 2 * d:])
    if paged:
        q_ref, kb_ref, vb_ref, kt_ref, vt_ref = rest
        kb_ref[...] = k.astype(BF16)
        vb_ref[...] = v.astype(BF16)
        for t_ref, x in ((kt_ref, k), (vt_ref, v)):
            xt = x.T
            for p in range(tm // PAGE_SIZE):
                t_ref[p] = xt[:, p * PAGE_SIZE:(p + 1) * PAGE_SIZE]
    else:
        q_ref, k_ref, v_ref = rest
        k_ref[...] = k
        v_ref[...] = v
    q_ref[...] = q.astype(q_ref.dtype)


def _qkv(x, g, w_qkv, *, tile_m, paged):
    n, d = x.shape
    assert w_qkv.shape == (d, 3 * d) and d == N_HEADS * HEAD_DIM
    tm = min(tile_m, n)
    assert n % tm == 0
    tok = pl.BlockSpec((tm, d), lambda i: (i, 0))
    f32o, bf16o = jax.ShapeDtypeStruct((n, d), F32), jax.ShapeDtypeStruct((n, d), BF16)
    args = [x, g.reshape(1, d), w_qkv.astype(BF16)]
    in_specs = [tok, _resident((1, d)), _resident((d, 3 * d))]
    if paged:
        assert tm % PAGE_SIZE == 0
        pages = pl.BlockSpec((tm // PAGE_SIZE, d, PAGE_SIZE), lambda i: (i, 0, 0))
        paged_o = jax.ShapeDtypeStruct((n // PAGE_SIZE, d, PAGE_SIZE), F32)
        out_shape, out_specs = [bf16o, bf16o, bf16o, paged_o, paged_o], [tok, tok, tok, pages, pages]
    else:
        out_shape, out_specs = [f32o, f32o, f32o], [tok, tok, tok]
    return pl.pallas_call(
        functools.partial(_qkv_kernel, paged=paged),
        grid=(n // tm,),
        in_specs=in_specs,
        out_specs=out_specs,
        out_shape=out_shape,
        compiler_params=pltpu.CompilerParams(dimension_semantics=("parallel",), vmem_limit_bytes=VMEM_LIMIT),
        name="qkv",
    )(*args)


def _sb_prompt_step(hp, qi, bias_ref, q_ref, k_ref, v_ref, o_ref,
                    qq_ref, tri_ref, u_ref, h_ref, rs_ref, r_ref, acc_ref, *, tq, tk):
    lane = lax.broadcasted_iota(jnp.int32, (1, LANES), 1)
    zero = jnp.zeros((tk, LANES), BF16)
    for c in range(4):
        qc = q_ref[(c // 2) * tk:(c // 2 + 1) * tk, :]
        qq_ref[c * tk:(c + 1) * tk] = jnp.where((lane >= HEAD_DIM) if c % 2 else (lane < HEAD_DIM), qc, zero)
    tri_ref[...] = (lax.broadcasted_iota(jnp.int32, (tk, tk), 0)
                    > lax.broadcasted_iota(jnp.int32, (tk, tk), 1)).astype(BF16)
    r_ref[...] = jnp.zeros_like(r_ref)
    acc_ref[...] = jnp.zeros_like(acc_ref)
    bias = (bias_ref[2 * hp], bias_ref[2 * hp + 1])

    def stage_a(kb, slot, masked, r0=0):
        start = pl.multiple_of(kb * tk, tk)
        z = _dot_nt(qq_ref[r0:], k_ref[pl.ds(start, tk), :])
        for c in range(r0 // tk, 4):
            rows = slice(c * tk, (c + 1) * tk)
            zc = z[c * tk - r0:(c + 1) * tk - r0] + bias[c % 2]
            sp = _softplus(zc)
            u = zc - sp
            if masked:
                q_pos = qi * tq + (c // 2) * tk + lax.broadcasted_iota(jnp.int32, (tk, tk), 0)
                k_pos = kb * tk + lax.broadcasted_iota(jnp.int32, (tk, tk), 1)
                causal = k_pos < q_pos
                sp = jnp.where(causal, sp, 0.0)
                u = jnp.where(causal, u, -jnp.inf)
            u_ref[slot, rows] = u
            h_ref[slot, rows] = sp.astype(BF16)
            rs_ref[slot, rows] = jnp.broadcast_to(jnp.sum(sp, axis=-1, keepdims=True), (tk, LANES))

    def stage_b(kb, slot, r0=0):
        start = pl.multiple_of(kb * tk, tk)
        after = _dot(h_ref[slot, r0:], tri_ref[...])
        r = r_ref[r0:]
        e = u_ref[slot, r0:] - after - jnp.concatenate([r] * (tk // LANES), axis=1)
        acc_ref[r0:] += _dot(jnp.exp(e).astype(BF16), v_ref[pl.ds(start, tk), :])
        r_ref[r0:] = r + rs_ref[slot, r0:]

    top = 2 * qi + 1
    stage_a(top, 0, True, r0=tq)
    stage_a(top - 1, 1, True)
    stage_b(top, 0, r0=tq)

    def pair(kb):
        stage_a(kb, 0, False)
        stage_b(kb + 1, 1)
        stage_a(kb - 1, 1, False)
        stage_b(kb, 0)

    odd = qi % 2

    @pl.when(odd == 1)
    def _():
        pair(2 * qi - 1)

    def quad(t, carry):
        kb = 2 * (qi - odd) - 1 - 4 * t
        pair(kb)
        pair(kb - 2)
        return carry

    lax.fori_loop(0, qi // 2, quad, 0)
    stage_b(0, 1)
    for half in range(2):
        lo, hi = acc_ref[2 * half * tk:(2 * half + 1) * tk], acc_ref[(2 * half + 1) * tk:(2 * half + 2) * tk]
        o_ref[half * tk:(half + 1) * tk, :] = jnp.where(lane < HEAD_DIM, lo, hi).astype(o_ref.dtype)


def _page_copies(step, slot, pt_ref, kpool_ref, vpool_ref, kbuf_ref, vbuf_ref, sem_ref, *, n_chunks, pages):
    n_pages = n_chunks * pages
    seq = step // n_chunks
    first = (n_chunks - 1 - step % n_chunks) * pages
    out = []
    for p in range(pages):
        page = pt_ref[seq * n_pages + first + p]
        out.append(pltpu.make_async_copy(kpool_ref.at[page], kbuf_ref.at[slot, p], sem_ref.at[0, slot]))
        out.append(pltpu.make_async_copy(vpool_ref.at[page], vbuf_ref.at[slot, p], sem_ref.at[1, slot]))
    return out


def _sb_sample_prefetch(n, n_steps, *dma_refs, n_chunks, pages):
    copies = functools.partial(_page_copies, n_chunks=n_chunks, pages=pages)

    @pl.when(n == 0)
    def _():
        for cp in copies(n, n % 2, *dma_refs):
            cp.start()

    @pl.when(n + 1 < n_steps)
    def _():
        for cp in copies(n + 1, 1 - n % 2, *dma_refs):
            cp.start()


def _sb_sample_step(n, pt_ref, bias_ref, q_ref, kn_ref, vn_ref, kpool_ref, vpool_ref, o_ref,
                    kbuf_ref, vbuf_ref, pad_ref, qbd_ref, r_ref, acc_ref, sem_ref, *, n_chunks, pages):
    ts, d = q_ref.shape
    hq = N_HEADS * ts
    blk = MXU_DIM
    slot = n % 2
    row = lax.broadcasted_iota(jnp.int32, (blk, blk), 0)
    col = lax.broadcasted_iota(jnp.int32, (blk, blk), 1)
    tri = (row > col).astype(BF16)

    def fold(z, valid, times_v):
        w = z.shape[1]
        step = min(blk, w)
        sp = _softplus(z)
        u = z - sp
        if valid is not None:
            sp = jnp.where(valid, sp, 0.0)
            u = jnp.where(valid, u, -jnp.inf)
        r = r_ref[...]
        es = [None] * (w // step)
        for j in reversed(range(w // step)):
            spj = sp[:, j * step:(j + 1) * step]
            after = _dot(spj.astype(BF16), tri[:step, :step])
            es[j] = u[:, j * step:(j + 1) * step] - after - r
            r = r + jnp.sum(spj, axis=-1, keepdims=True)
        acc_ref[...] += times_v(jnp.exp(jnp.concatenate(es, axis=1)).astype(BF16))
        r_ref[...] = r

    @pl.when(n % n_chunks == 0)
    def _():
        q = jnp.concatenate([q_ref[...]] * N_HEADS, axis=0)
        qr = lax.broadcasted_iota(jnp.int32, (hq, d), 0)
        qc = lax.broadcasted_iota(jnp.int32, (hq, d), 1)
        qbd_ref[...] = jnp.where(qr // ts == qc // HEAD_DIM, q, 0.0).astype(BF16)
        r_ref[...] = jnp.zeros_like(r_ref)
        acc_ref[...] = jnp.zeros_like(acc_ref)
        pad_ref[...] = jnp.zeros_like(pad_ref)
        pad_ref[0, 0:ts, :] = kn_ref[...]
        pad_ref[1, 0:ts, :] = vn_ref[...]
        qry = lax.broadcasted_iota(jnp.int32, (hq, PAGE_SIZE), 0) % ts
        key = lax.broadcasted_iota(jnp.int32, (hq, PAGE_SIZE), 1)
        z = _dot_nt(qbd_ref[...], pad_ref[0].astype(BF16)) + bias_ref[...]
        fold(z, key < qry, lambda a: _dot(a, pad_ref[1].astype(BF16)))

    for cp in _page_copies(n, slot, pt_ref, kpool_ref, vpool_ref, kbuf_ref, vbuf_ref, sem_ref,
                           n_chunks=n_chunks, pages=pages):
        cp.wait()
    all_pages = lambda buf: jnp.concatenate([buf[slot, p] for p in range(pages)], axis=1).astype(BF16)
    z = _dot(qbd_ref[...], all_pages(kbuf_ref)) + bias_ref[...]
    fold(z, None, lambda a: _dot_nt(a, all_pages(vbuf_ref)))

    @pl.when(n % n_chunks == n_chunks - 1)
    def _():
        orow = lax.broadcasted_iota(jnp.int32, (hq, d), 0)
        ocol = lax.broadcasted_iota(jnp.int32, (hq, d), 1)
        o = jnp.where(orow // ts == ocol // HEAD_DIM, acc_ref[...], 0.0)
        o_ref[...] = jnp.sum(o.reshape(N_HEADS, ts, d), axis=0)


def _sb_attention_kernel(pt_ref, pbias_ref, qp_ref, kp_ref, vp_ref, sbias_ref, qs_ref, kn_ref, vn_ref,
                         kpool_ref, vpool_ref, op_ref, os_ref,
                         qq_ref, tri_ref, u_ref, h_ref, rs_ref, r_ref, acc_ref,
                         kbuf_ref, vbuf_ref, pad_ref, qbd_ref, sr_ref, sacc_ref, sem_ref,
                         *, tq, tk, n_chunks, pages, n_sample_steps):
    hp = pl.program_id(1)
    nq = qp_ref.shape[0] // tq
    first = (pl.program_id(0) * pl.num_programs(1) + hp) * nq

    def unit(qi, carry):
        n = first + qi
        _sb_sample_prefetch(n, n_sample_steps, pt_ref, kpool_ref, vpool_ref, kbuf_ref, vbuf_ref, sem_ref,
                            n_chunks=n_chunks, pages=pages)
        rows = pl.ds(pl.multiple_of(qi * tq, tq), tq)
        _sb_prompt_step(hp, qi, pbias_ref, qp_ref.at[rows], kp_ref, vp_ref, op_ref.at[rows],
                        qq_ref, tri_ref, u_ref, h_ref, rs_ref, r_ref, acc_ref, tq=tq, tk=tk)

        @pl.when(n < n_sample_steps)
        def _():
            seq = n // n_chunks
            _sb_sample_step(n, pt_ref, sbias_ref, qs_ref.at[seq], kn_ref.at[seq], vn_ref.at[seq], kpool_ref, vpool_ref,
                            os_ref.at[seq], kbuf_ref, vbuf_ref, pad_ref, qbd_ref, sr_ref, sacc_ref, sem_ref,
                            n_chunks=n_chunks, pages=pages)
        return carry

    lax.fori_loop(0, nq, unit, 0)


def _sb_attention(qp, kp, vp, qs, k_new, v_new, k_pool, v_pool, page_table, bias, *, tq, tk, pages):
    b, t, d = qp.shape
    nb, ts, _ = qs.shape
    n_pages = page_table.shape[1]
    assert t % tq == 0 and tq == 2 * tk and tk % LANES == 0
    assert d == N_HEADS * HEAD_DIM and 2 * HEAD_DIM == LANES
    assert ts == SUBLANES and N_HEADS * ts == LANES and n_pages % pages == 0 and (pages * PAGE_SIZE) % MXU_DIM == 0
    assert k_pool.shape[1:] == (PAGE_SIZE, N_HEADS, HEAD_DIM)
    n_chunks = n_pages // pages
    n_pool = k_pool.shape[0]
    hq = N_HEADS * ts
    hps, nq = N_HEADS // 2, t // tq
    n_sample_steps = nb * n_chunks
    assert n_sample_steps <= b * hps * nq, "more sample chunks than prompt grid steps"
    pool_t = lambda a: jnp.transpose(a, (0, 2, 3, 1)).reshape(n_pool, d, PAGE_SIZE)
    pair_spec = pl.BlockSpec((None, t, LANES), lambda i, h, pt: (i, 0, h))
    tok = pl.BlockSpec((nb, ts, d), lambda i, h, pt: (0, 0, 0))
    m = 2 * tq
    grid_spec = pltpu.PrefetchScalarGridSpec(
        num_scalar_prefetch=1,
        grid=(b, hps),
        in_specs=[pl.BlockSpec(memory_space=pltpu.SMEM), pair_spec, pair_spec, pair_spec,
                  pl.BlockSpec((hq, 1), lambda i, h, pt: (0, 0)), tok, tok, tok,
                  pl.BlockSpec(memory_space=pl.ANY), pl.BlockSpec(memory_space=pl.ANY)],
        out_specs=[pair_spec, tok],
        scratch_shapes=[pltpu.VMEM((m, LANES), BF16), pltpu.VMEM((tk, tk), BF16),
                        pltpu.VMEM((2, m, tk), F32), pltpu.VMEM((2, m, tk), BF16),
                        pltpu.VMEM((2, m, LANES), F32), pltpu.VMEM((m, LANES), F32), pltpu.VMEM((m, LANES), F32),
                        pltpu.VMEM((2, pages, d, PAGE_SIZE), F32), pltpu.VMEM((2, pages, d, PAGE_SIZE), F32),
                        pltpu.VMEM((2, PAGE_SIZE, d), F32), pltpu.VMEM((hq, d), BF16),
                        pltpu.VMEM((hq, 1), F32), pltpu.VMEM((hq, d), F32),
                        pltpu.SemaphoreType.DMA((2, 2))],
    )
    return pl.pallas_call(
        functools.partial(_sb_attention_kernel, tq=tq, tk=tk, n_chunks=n_chunks, pages=pages,
                          n_sample_steps=n_sample_steps),
        grid_spec=grid_spec,
        out_shape=[jax.ShapeDtypeStruct((b, t, d), BF16), jax.ShapeDtypeStruct((nb, ts, d), F32)],
        compiler_params=pltpu.CompilerParams(dimension_semantics=("arbitrary", "arbitrary"),
                                             vmem_limit_bytes=VMEM_LIMIT),
        name="sb_attention",
    )(page_table.reshape(-1), bias.astype(F32), qp, kp, vp,
      jnp.repeat(bias.astype(F32), ts).reshape(hq, 1), qs, k_new, v_new, pool_t(k_pool), pool_t(v_pool))


def kernel(x_prompt, x_sample, cache_conv, cache_k, cache_v, page_table, mix_norm_g, ffn_norm_g, final_norm_g,
           cv_w_pw1, cv_b_pw1, cv_w_dw, cv_b_dw, cv_ln_g, cv_ln_b, cv_w_pw2, cv_b_pw2,
           sb_w_qkv, sb_w_o, sb_logit_bias, ffn_w_gate, ffn_w_up, ffn_w_down):
    bp, tp, d = x_prompt.shape
    bs, ts, _ = x_sample.shape
    assert mix_norm_g.shape[0] == 2 and tp % PAGE_SIZE == 0

    cw = _conv_weights(mix_norm_g[0], cv_w_pw1[0], cv_b_pw1[0], cv_w_dw[0], cv_b_dw[0], cv_ln_g[0], cv_ln_b[0],
                       cv_w_pw2[0], cv_b_pw2[0])
    xp, st_p = _conv_prompt(x_prompt, cw, tile_t=256)
    xs, st_s = _conv_sample(x_sample, cache_conv[0], cw)
    ffn_w = (ffn_w_gate.astype(BF16), ffn_w_up.astype(BF16), ffn_w_down.astype(BF16))
    xp = _ffn(xp.reshape(bp * tp, d), ffn_norm_g[0], *ffn_w, 0, tile_m=512)
    xs = _ffn(xs.reshape(bs * ts, d), ffn_norm_g[0], *ffn_w, 0, tile_m=512)

    qp, kpb, vpb, kp_t, vp_t = _qkv(xp, mix_norm_g[1], sb_w_qkv[0], tile_m=512, paged=True)
    qs, ks, vs = _qkv(xs, mix_norm_g[1], sb_w_qkv[0], tile_m=512, paged=False)
    seq = lambda a: a.reshape(bp, tp, d)
    dec = lambda a: a.reshape(bs, ts, d)
    op, os_ = _sb_attention(seq(qp), seq(kpb), seq(vpb), dec(qs), dec(ks), dec(vs), cache_k[0], cache_v[0],
                            page_table, sb_logit_bias[0], tq=512, tk=256, pages=16)
    yp = _ffn(xp, ffn_norm_g[1], *ffn_w, 1, tile_m=512, proj=(op.reshape(bp * tp, d), sb_w_o[0]), final_g=final_norm_g)
    ys = _ffn(xs, ffn_norm_g[1], *ffn_w, 1, tile_m=512, proj=(os_.reshape(bs * ts, d), sb_w_o[0]), final_g=final_norm_g)

    pages_p = lambda a: jnp.transpose(a.reshape(1, bp, tp // PAGE_SIZE, N_HEADS, HEAD_DIM, PAGE_SIZE), (0, 1, 2, 5, 3, 4))
    new_s = (1, bs, ts, N_HEADS, HEAD_DIM)
    return (yp.reshape(bp, tp, d), ys.reshape(bs, ts, d), st_p[None], st_s[None],
            pages_p(kp_t), pages_p(vp_t), ks.reshape(new_s), vs.reshape(new_s))
```
